```python
import functools
import jax, jax.numpy as jnp
from jax import lax
import numpy as np

D_MODEL = 1024
BATCH = 8
SEQ = 2048
DEPTH = 4
DEC_BATCH = 128
DEC_SEQ = 4
PAST_LEN = 16384
PAGE_SIZE = 128

MIX_WIDTH = D_MODEL
HG_HEADS = 4
HG_DK = MIX_WIDTH // 2 // HG_HEADS
HG_DV = HG_DK
RT_HEADS = 4
RT_DK = MIX_WIDTH // 2 // RT_HEADS
RT_DV = RT_DK
HG_W = HG_HEADS * HG_DK
RT_W = RT_HEADS * RT_DK
PROJ_COLS = 4 * HG_W + 4 * RT_W
SPLITS = [HG_W, 2 * HG_W, 3 * HG_W, 4 * HG_W, 4 * HG_W + RT_W, 4 * HG_W + 2 * RT_W, 4 * HG_W + 3 * RT_W]
N_META = 16
CHUNK = 64
CONV_W = 3
D_FF = 2816
ROPE_BASE = 10000.0
LN_EPS = 1e-5
F_EPS = 1e-6
ALPHA = (2 * DEPTH) ** 0.25
BETA = (8 * DEPTH) ** -0.25

kernel_name = 'hymba_hgrn2_retention_convffn_step'


def layer_norm(x, g, b):
    xf = x.astype(jnp.float32)
    mu = jnp.mean(xf, -1, keepdims=True)
    var = jnp.mean(jnp.square(xf - mu), -1, keepdims=True)
    return ((xf - mu) * lax.rsqrt(var + LN_EPS) * g + b).astype(x.dtype)


def head_rms_norm(o):
    return o * lax.rsqrt(jnp.mean(jnp.square(o), -1, keepdims=True) + LN_EPS)


def head_group_norm(o):
    mu = jnp.mean(o, -1, keepdims=True)
    var = jnp.mean(jnp.square(o - mu), -1, keepdims=True)
    return (o - mu) * lax.rsqrt(var + LN_EPS)


def rotary(x, pos):
    half = x.shape[-1] // 2
    inv = ROPE_BASE ** (-jnp.arange(half, dtype=jnp.float32) / half)
    ang = pos.astype(jnp.float32)[:, None] * inv[None, :]
    cos = jnp.cos(ang)[None, :, None, :]
    sin = jnp.sin(ang)[None, :, None, :]
    x1, x2 = x[..., :half], x[..., half:]
    return jnp.concatenate([x1 * cos - x2 * sin, x1 * sin + x2 * cos], -1)


def hgrn2_block(s0, q, k, v, log_f):
    c = q.shape[1]
    b = jnp.cumsum(log_f, axis=1)
    causal = jnp.tril(jnp.ones((c, c), bool))[None, :, :, None, None]
    diff = jnp.where(causal, b[:, :, None] - b[:, None, :], 0.0)
    decay = jnp.where(causal, jnp.exp(diff), 0.0)
    scores = jnp.einsum('bthk,btshk,bshk->bhts', q, decay, k)
    o = jnp.einsum('bhts,bshv->bthv', scores, v) + jnp.einsum('bthk,bhkv->bthv', q * jnp.exp(b), s0)
    b_last = b[:, -1]
    s_new = jnp.exp(b_last)[..., None] * s0 + jnp.einsum('bshk,bshv->bhkv', k * jnp.exp(b_last[:, None] - b), v)
    return s_new, o


def retention_block(s0, q, k, v):
    c = q.shape[1]
    log_gamma = jnp.log1p(-jnp.exp2(-5.0 - jnp.arange(RT_HEADS, dtype=jnp.float32)))
    idx = jnp.arange(c, dtype=jnp.float32)
    rel = idx[:, None] - idx[None, :]
    mask = (rel >= 0)[None]
    decay = jnp.where(mask, jnp.exp(jnp.where(mask, rel[None] * log_gamma[:, None, None], 0.0)), 0.0)
    scores = jnp.einsum('bthk,bshk->bhts', q, k) * decay[None]
    q_dec = jnp.exp((idx + 1.0)[:, None] * log_gamma[None, :])
    o = jnp.einsum('bhts,bshv->bthv', scores, v) + jnp.einsum('bthk,bhkv->bthv', q, s0) * q_dec[None, :, :, None]
    k_dec = jnp.exp((c - 1.0 - idx)[:, None] * log_gamma[None, :])
    s_new = jnp.exp(c * log_gamma)[None, :, None, None] * s0 + jnp.einsum('bshk,bshv->bhkv', k * k_dec[None, :, :, None], v)
    return s_new, o


def mix_sequence(block_fn, s0, arrays, lead):
    s, o_lead = block_fn(s0, *(a[:, :lead] for a in arrays))
    rest = tuple(a[:, lead:] for a in arrays)
    n_blocks = rest[0].shape[1] // CHUNK
    if n_blocks == 0:
        return s, o_lead

    def to_blocks(a):
        return jnp.swapaxes(a.reshape(a.shape[0], n_blocks, CHUNK, *a.shape[2:]), 0, 1)

    s, o_rest = lax.scan(lambda st, xs: block_fn(st, *xs), s, tuple(to_blocks(a) for a in rest))
    o_rest = jnp.swapaxes(o_rest, 0, 1).reshape(o_lead.shape[0], n_blocks * CHUNK, *o_lead.shape[2:])
    return s, jnp.concatenate([o_lead, o_rest], axis=1)


def token_mixer(x, pos, s_hg, s_rt, lead, lb, w_in, hg_norm_g, rt_gn_g, rt_gn_b, w_o):
    bsz, t, _ = x.shape
    f32 = jnp.float32
    proj = jnp.einsum('btd,dc->btc', x, w_in).astype(f32)
    hq, hf, hv, hg, rq, rk, rv, rg = jnp.split(proj, SPLITS, axis=-1)

    def heads(a, h):
        return a.reshape(bsz, t, h, -1)

    f_gate = lb + (1.0 - lb) * jax.nn.sigmoid(hf)
    log_f = jnp.log(jnp.maximum(f_gate, F_EPS))
    k_hg = (1.0 - lb) * jax.nn.sigmoid(-hf)
    s_hg, o_hg = mix_sequence(hgrn2_block, s_hg.astype(f32),
                              (heads(jax.nn.silu(hq), HG_HEADS), heads(k_hg, HG_HEADS),
                               heads(hv, HG_HEADS), heads(log_f, HG_HEADS)), lead)
    o_hg = head_rms_norm(o_hg).reshape(bsz, t, HG_W) * hg_norm_g * jax.nn.silu(hg)

    q_r = rotary(heads(rq, RT_HEADS), pos)
    k_r = rotary(heads(rk, RT_HEADS), pos) * (RT_DK ** -0.5)
    s_rt, o_rt = mix_sequence(retention_block, s_rt.astype(f32), (q_r, k_r, heads(rv, RT_HEADS)), lead)
    o_rt = (head_group_norm(o_rt).reshape(bsz, t, RT_W) * rt_gn_g + rt_gn_b) * jax.nn.silu(rg)

    o = jnp.concatenate([o_hg, o_rt], axis=-1).astype(x.dtype)
    return jnp.einsum('btc,cd->btd', o, w_o), s_hg, s_rt


def conv_ffn(x, conv_buf, w_a, w_b, conv_w, conv_b, w_down):
    a = jnp.einsum('btd,df->btf', x, w_a)
    g = jnp.einsum('btd,df->btf', x, w_b)
    t = a.shape[1]
    a_ext = jnp.concatenate([conv_buf.astype(a.dtype), a], axis=1)
    conv = conv_b + sum(conv_w[j] * a_ext[:, j:j + t] for j in range(CONV_W))
    h = jax.nn.gelu(conv) * g
    return jnp.einsum('btf,fd->btd', h, w_down), a_ext[:, -(CONV_W - 1):]


def trunk(x, pos, lead, s_hg, s_rt, s_conv, lb_all, w_in, hg_norm_g, rt_gn_g, rt_gn_b, w_o,
          ln1_g, ln1_b, w_a, w_b, conv_w, conv_b, w_down, ln2_g, ln2_b):
    new_hg, new_rt, new_conv = [], [], []
    for l in range(DEPTH):
        mix, h_state, r_state = token_mixer(x, pos, s_hg[l], s_rt[l], lead, lb_all[l], w_in[l],
                                            hg_norm_g[l], rt_gn_g[l], rt_gn_b[l], w_o[l])
        x = layer_norm(ALPHA * x + mix, ln1_g[l], ln1_b[l])
        ffn, c_state = conv_ffn(x, s_conv[l], w_a[l], w_b[l], conv_w[l], conv_b[l], w_down[l])
        x = layer_norm(ALPHA * x + ffn, ln2_g[l], ln2_b[l])
        new_hg.append(h_state)
        new_rt.append(r_state)
        new_conv.append(c_state)
    return x, jnp.stack(new_hg), jnp.stack(new_rt), jnp.stack(new_conv)


def setup_inputs(seed: int = 0) -> dict:
    key = jax.random.key(seed)
    ks = jax.random.split(key, 24)
    f32 = jnp.float32

    def nrm(k, shape, s):
        return jax.random.normal(k, shape, f32) * s

    col_scale = jnp.concatenate([jnp.ones((2 * HG_W,), f32), jnp.full((HG_W,), BETA, f32), jnp.ones((HG_W,), f32),
                                 jnp.ones((2 * RT_W,), f32), jnp.full((RT_W,), BETA, f32), jnp.ones((RT_W,), f32)])
    return {
        'x_prompt': nrm(ks[0], (BATCH, SEQ, D_MODEL), 1.0),
        'x_sample': nrm(ks[1], (DEC_BATCH, DEC_SEQ, D_MODEL), 1.0),
        'state_hgrn': nrm(ks[2], (DEPTH, DEC_BATCH, HG_HEADS, HG_DK, HG_DV), 0.5),
        'state_ret': nrm(ks[3], (DEPTH, DEC_BATCH, RT_HEADS, RT_DK, RT_DV), 1.0),
        'state_conv': nrm(ks[4], (DEPTH, DEC_BATCH, CONV_W - 1, D_FF), 1.0),
        'meta_tokens': nrm(ks[5], (N_META, D_MODEL), 1.0),
        'emb_ln_g': 1.0 + nrm(ks[6], (D_MODEL,), 0.02),
        'emb_ln_b': nrm(ks[7], (D_MODEL,), 0.02),
        'lb_logits': nrm(ks[8], (DEPTH, HG_W), 0.5),
        'w_in': nrm(ks[9], (DEPTH, D_MODEL, PROJ_COLS), D_MODEL ** -0.5) * col_scale,
        'hg_norm_g': 1.0 + nrm(ks[10], (DEPTH, HG_W), 0.02),
        'rt_gn_g': 1.0 + nrm(ks[11], (DEPTH, RT_W), 0.02),
        'rt_gn_b': nrm(ks[12], (DEPTH, RT_W), 0.02),
        'w_o': nrm(ks[13], (DEPTH, MIX_WIDTH, D_MODEL), MIX_WIDTH ** -0.5 * BETA),
        'ln1_g': 1.0 + nrm(ks[14], (DEPTH, D_MODEL), 0.02),
        'ln1_b': nrm(ks[15], (DEPTH, D_MODEL), 0.02),
        'w_a': nrm(ks[16], (DEPTH, D_MODEL, D_FF), D_MODEL ** -0.5),
        'w_b': nrm(ks[17], (DEPTH, D_MODEL, D_FF), D_MODEL ** -0.5),
        'conv_w': nrm(ks[18], (DEPTH, CONV_W, D_FF), CONV_W ** -0.5),
        'conv_b': nrm(ks[19], (DEPTH, D_FF), 0.02),
        'w_down': nrm(ks[20], (DEPTH, D_FF, D_MODEL), D_FF ** -0.5 * BETA),
        'ln2_g': 1.0 + nrm(ks[21], (DEPTH, D_MODEL), 0.02),
        'ln2_b': nrm(ks[22], (DEPTH, D_MODEL), 0.02),
    }


def reference(x_prompt, x_sample, state_hgrn, state_ret, state_conv, meta_tokens, emb_ln_g, emb_ln_b,
              lb_logits, w_in, hg_norm_g, rt_gn_g, rt_gn_b, w_o, ln1_g, ln1_b, w_a, w_b, conv_w, conv_b,
              w_down, ln2_g, ln2_b):
    sm = jax.nn.softmax(lb_logits.astype(jnp.float32), axis=0)
    lb_all = jnp.cumsum(sm, axis=0) - sm[0:1]
    weights = (lb_all, w_in, hg_norm_g, rt_gn_g, rt_gn_b, w_o, ln1_g, ln1_b,
               w_a, w_b, conv_w, conv_b, w_down, ln2_g, ln2_b)

    bsz, seq = x_prompt.shape[0], x_prompt.shape[1]
    meta = jnp.broadcast_to(meta_tokens.astype(x_prompt.dtype)[None], (bsz, N_META, D_MODEL))
    xp = layer_norm(jnp.concatenate([meta, x_prompt], axis=1), emb_ln_g, emb_ln_b)
    pos_p = jnp.arange(N_META + seq, dtype=jnp.int32)
    zero_hg = jnp.zeros((DEPTH, bsz, HG_HEADS, HG_DK, HG_DV), jnp.float32)
    zero_rt = jnp.zeros((DEPTH, bsz, RT_HEADS, RT_DK, RT_DV), jnp.float32)
    zero_conv = jnp.zeros((DEPTH, bsz, CONV_W - 1, D_FF), x_prompt.dtype)
    yp, hg_p, rt_p, conv_p = trunk(xp, pos_p, N_META, zero_hg, zero_rt, zero_conv, *weights)
    y_prompt = yp[:, N_META:]

    xs = layer_norm(x_sample, emb_ln_g, emb_ln_b)
    pos_s = PAST_LEN + jnp.arange(x_sample.shape[1], dtype=jnp.int32)
    y_sample, hg_s, rt_s, conv_s = trunk(xs, pos_s, x_sample.shape[1], state_hgrn, state_ret, state_conv, *weights)

    return (y_prompt, y_sample, hg_p, hg_s, rt_p, rt_s, conv_p, conv_s)
```

```python
import functools
import math
from typing import NamedTuple

import numpy as np
import jax
import jax.numpy as jnp
from jax import lax
from jax.experimental import pallas as pl
from jax.experimental.pallas import tpu as pltpu

D_MODEL = 1024
DEPTH = 4
HEADS = 4
DK = 128
HW = HEADS * DK
PROJ_COLS = 8 * HW
N_META = 16
PAST_LEN = 16384
CONV_W = 3
D_FF = 2816
ROPE_BASE = 10000.0
LN_EPS = 1e-5
F_EPS = 1e-6
ALPHA = (2 * DEPTH) ** 0.25

TILE = 128
CARRY = 8
VMEM_LIMIT_V7X = 56 * 1024 * 1024

F32 = jnp.float32
BF16 = jnp.bfloat16
NT_DIMS = (((1,), (1,)), ((), ()))
TN_DIMS = (((0,), (0,)), ((), ()))


class Cfg(NamedTuple):
    nb: int
    nt: int
    C: int
    L: int
    Lv: int
    embed: bool
    per_seq: bool
    layer: int


def _levels(Lv):
    return [m for m in (1, 2, 4, 8, 16, 32, 64, 128, 256) if m < Lv]


@functools.lru_cache(maxsize=None)
def _tables(C, L, Lv):
    r = np.arange(C)
    g, p = r // L, r % L
    same = g[:, None] == g[None, :]
    pt, pu = p[:, None], p[None, :]
    blocks = [same & (pu <= pt), same & (pu > pt)]
    lv = np.full((C, C), -1, np.int32)
    ok = same & (pt < Lv) & (pu < Lv)
    lv[ok & (pt == pu)] = 0
    for j, m in enumerate(_levels(Lv)):
        blk_t, blk_u = pt // (2 * m), pu // (2 * m)
        half_t, half_u = (pt // m) % 2, (pu // m) % 2
        inblk = same & (blk_t == blk_u)
        blocks.append(inblk & (((half_t == 1) & (half_u == 1) & (pu <= pt)) |
                               ((half_t == 0) & (half_u == 0) & (pu > pt))))
        lv[ok & inblk & (half_t == 1) & (half_u == 0)] = j + 1
    a_stack = np.concatenate(blocks, 0).astype(np.float32)

    log_gamma = np.log1p(-np.exp2(-5.0 - np.arange(HEADS, dtype=np.float64)))
    rel = (pt - pu).astype(np.float64)
    causal = same & (pu <= pt) & (pu < Lv)
    dec = np.where(causal[None], np.exp(np.where(causal, rel, 0.0)[None] * log_gamma[:, None, None]), 0.0)
    qdec = np.exp((p + 1.0)[:, None] * log_gamma[None, :])
    kdec = np.where((p < Lv)[:, None], np.exp((Lv - 1.0 - p)[:, None] * log_gamma[None, :]), 0.0)
    qdec = np.repeat(qdec, DK, axis=1)
    kdec = np.repeat(kdec, DK, axis=1)
    g_total = tuple(float(x) for x in np.exp(Lv * log_gamma))
    return (a_stack, lv, dec.astype(np.float32), qdec.astype(np.float32), kdec.astype(np.float32), g_total)


def _ln(x, g, b):
    mu = jnp.mean(x, -1, keepdims=True)
    xc = x - mu
    var = jnp.mean(xc * xc, -1, keepdims=True)
    return xc * lax.rsqrt(var + LN_EPS) * g + b


def _silu(x):
    return x * jax.nn.sigmoid(x)


def _expand(x, gs, mask):
    if gs == 1:
        return x
    return jnp.where(mask, jnp.concatenate([x] * gs, axis=1), 0.0)


def _col_bcast(rows, gs):
    outs = []
    for g in range(gs):
        outs.append(jnp.broadcast_to(rows[g:g + 1, :], (DK, DK)).T)
    return outs[0] if gs == 1 else jnp.concatenate(outs, axis=0)


def _mix_kernel(cfg, g_total, x_ref, eg_ref, eb_ref, win_ref, wo_ref, lb_ref, hgn_ref, rtg_ref, rtb_ref,
                l1g_ref, l1b_ref, cos_ref, sin_ref, a_ref, lv_ref, dec_ref, qdec_ref, kdec_ref,
                hg0_ref, rt0_ref, x1_ref, hgn_out, rtn_out, hg_s, rt_s):
    C, L, Lv = cfg.C, cfg.L, cfg.Lv
    gs = C // L
    levels = _levels(Lv)
    t = pl.program_id(1)

    @pl.when(t == 0)
    def _():
        for h in range(HEADS):
            hg_s[h] = hg0_ref[:, h].reshape(gs * DK, DK)
            rt_s[h] = rt0_ref[:, h].reshape(gs * DK, DK)

    x = x_ref[...]
    if cfg.embed:
        x = _ln(x, eg_ref[...], eb_ref[...])
    proj = jnp.dot(x.astype(BF16), win_ref[...], preferred_element_type=F32)

    if gs > 1:
        row_seq = lax.broadcasted_iota(jnp.int32, (C, gs * DK), 0) // L
        lane_blk = lax.broadcasted_iota(jnp.int32, (C, gs * DK), 1) // DK
        seq_mask = row_seq == lane_blk
    else:
        seq_mask = None
    lv = lv_ref[...]

    hq, hf = proj[:, 0:HW], proj[:, HW:2 * HW]
    hv, hgate = proj[:, 2 * HW:3 * HW], proj[:, 3 * HW:4 * HW]
    lb = lb_ref[...]
    f_gate = lb + (1.0 - lb) * jax.nn.sigmoid(hf)
    lf = jnp.log(jnp.maximum(f_gate, F_EPS))
    kx = (1.0 - lb) * jax.nn.sigmoid(-hf)
    q = _silu(hq)
    if Lv < L:
        real = (lax.broadcasted_iota(jnp.int32, (C, 1), 0) % L) < Lv
        lf = jnp.where(real, lf, 0.0)
        kx = jnp.where(real, kx, 0.0)
    a_mat = a_ref[...]
    lf_hi = lf.astype(BF16)
    rem = lf - lf_hi.astype(F32)
    lf_mid = rem.astype(BF16)
    lf_lo = (rem - lf_mid.astype(F32)).astype(BF16)
    args = (jnp.dot(a_mat, lf_hi, preferred_element_type=F32)
            + jnp.dot(a_mat, lf_mid, preferred_element_type=F32)
            + jnp.dot(a_mat, lf_lo, preferred_element_type=F32))
    b_incl = args[0:C]
    q_in = q * jnp.exp(b_incl)
    k_out = kx * jnp.exp(args[C:2 * C])
    if gs == 1:
        b_last = b_incl[Lv - 1:Lv, :]
    else:
        b_last = b_incl.reshape(gs, L, HW)[:, Lv - 1, :]
    d_rows = jnp.exp(b_last)

    outs = []
    for h in range(HEADS):
        sl = slice(h * DK, (h + 1) * DK)
        qh, kh = q[:, sl], kx[:, sl]
        vb = hv[:, sl].astype(BF16)
        sc = lax.dot_general(qh.astype(BF16), kh.astype(BF16), NT_DIMS, preferred_element_type=F32)
        acc = jnp.where(lv == 0, sc, 0.0)
        for j in range(len(levels)):
            e = jnp.exp(args[(2 + j) * C:(3 + j) * C, sl])
            sc = lax.dot_general((qh * e).astype(BF16), (kh * e).astype(BF16), NT_DIMS,
                                 preferred_element_type=F32)
            acc = jnp.where(lv == j + 1, sc, acc)
        s_old = hg_s[h]
        o = jnp.dot(acc.astype(BF16), vb, preferred_element_type=F32)
        o = o + jnp.dot(_expand(q_in[:, sl], gs, seq_mask).astype(BF16), s_old.astype(BF16),
                        preferred_element_type=F32)
        upd = lax.dot_general(_expand(k_out[:, sl], gs, seq_mask).astype(BF16), vb, TN_DIMS,
                              preferred_element_type=F32)
        hg_s[h] = _col_bcast(d_rows[:, sl], gs) * s_old + upd
        o = o * lax.rsqrt(jnp.mean(o * o, -1, keepdims=True) + LN_EPS)
        outs.append(o * hgn_ref[:, sl] * _silu(hgate[:, sl]))

    rq, rk = proj[:, 4 * HW:5 * HW], proj[:, 5 * HW:6 * HW]
    rv, rgate = proj[:, 6 * HW:7 * HW], proj[:, 7 * HW:8 * HW]
    cos, sin = cos_ref[...], sin_ref[...]
    qdec, kdec = qdec_ref[...], kdec_ref[...]
    for h in range(HEADS):
        sl = slice(h * DK, (h + 1) * DK)
        qh = rq[:, sl] * cos[:, sl] + pltpu.roll(rq[:, sl], DK // 2, 1) * sin[:, sl]
        kh = (rk[:, sl] * cos[:, sl] + pltpu.roll(rk[:, sl], DK // 2, 1) * sin[:, sl]) * (DK ** -0.5)
        vb = rv[:, sl].astype(BF16)
        sc = lax.dot_general(qh.astype(BF16), kh.astype(BF16), NT_DIMS, preferred_element_type=F32)
        sc = sc * dec_ref[h]
        s_old = rt_s[h]
        o = jnp.dot(sc.astype(BF16), vb, preferred_element_type=F32)
        o = o + jnp.dot(_expand(qh * qdec[:, sl], gs, seq_mask).astype(BF16), s_old.astype(BF16),
                        preferred_element_type=F32)
        upd = lax.dot_general(_expand(kh * kdec[:, sl], gs, seq_mask).astype(BF16), vb, TN_DIMS,
                              preferred_element_type=F32)
        rt_s[h] = g_total[h] * s_old + upd
        mu = jnp.mean(o, -1, keepdims=True)
        oc = o - mu
        o = oc * lax.rsqrt(jnp.mean(oc * oc, -1, keepdims=True) + LN_EPS)
        outs.append((o * rtg_ref[:, sl] + rtb_ref[:, sl]) * _silu(rgate[:, sl]))

    o_all = jnp.concatenate(outs, axis=1).astype(BF16)
    mix = jnp.dot(o_all, wo_ref[...], preferred_element_type=F32)
    x1_ref[...] = _ln(ALPHA * x + mix, l1g_ref[...], l1b_ref[...])

    @pl.when(t == cfg.nt - 1)
    def _():
        for h in range(HEADS):
            hgn_out[:, h] = hg_s[h].reshape(gs, DK, DK)
            rtn_out[:, h] = rt_s[h].reshape(gs, DK, DK)


def _ffn_kernel(cfg, x_ref, wa_ref, wb_ref, cw_ref, cb_ref, wd_ref, l2g_ref, l2b_ref, *rest):
    C, L, Lv = cfg.C, cfg.L, cfg.Lv
    gs = C // L
    if cfg.per_seq:
        fill1_ref, fill2_ref, y_ref, cs_ref, a_scr = rest
    else:
        init_ref, y_ref, cs_ref, a_scr = rest
    t = pl.program_id(1)

    x = x_ref[...]
    xb = x.astype(BF16)
    a = jnp.dot(xb, wa_ref[...], preferred_element_type=F32)
    gate = jnp.dot(xb, wb_ref[...], preferred_element_type=F32)

    if cfg.per_seq:
        a_scr[0:CARRY, :] = jnp.zeros((CARRY, D_FF), F32)
    else:
        @pl.when(t == 0)
        def _():
            a_scr[0:CARRY, :] = jnp.zeros((CARRY, D_FF), F32)
            a_scr[CARRY - 2:CARRY, :] = init_ref[0]
    a_scr[CARRY:CARRY + C, :] = a
    a_m1 = a_scr[CARRY - 1:CARRY - 1 + C, :]
    a_m2 = a_scr[CARRY - 2:CARRY - 2 + C, :]
    if cfg.per_seq:
        pos = lax.broadcasted_iota(jnp.int32, (C, 1), 0) % L
        a_m1 = jnp.where(pos >= 1, a_m1, fill1_ref[...])
        a_m2 = jnp.where(pos >= 2, a_m2, fill2_ref[...])
    cw = cw_ref[...]
    conv = cb_ref[...] + cw[0:1, :] * a_m2 + cw[1:2, :] * a_m1 + cw[2:3, :] * a
    hidden = (jax.nn.gelu(conv, approximate=True) * gate).astype(BF16)
    ffn = jnp.dot(hidden, wd_ref[...], preferred_element_type=F32)
    y_ref[...] = _ln(ALPHA * x + ffn, l2g_ref[...], l2b_ref[...])

    if cfg.per_seq:
        a_seq = a.reshape(gs, L, D_FF)
        cs_ref[0] = a_seq[:, Lv - 2, :]
        cs_ref[1] = a_seq[:, Lv - 1, :]
    else:
        @pl.when(t == cfg.nt - 1)
        def _():
            cs_ref[0] = a_scr[CARRY + Lv - 2:CARRY + Lv, :]
        a_scr[0:CARRY, :] = a_scr[C:C + CARRY, :]


def _const_spec(shape):
    nd = len(shape)
    return pl.BlockSpec(shape, lambda b, t: (0,) * nd, pipeline_mode=pl.Buffered(1))


def _layer_spec(shape, layer):
    nd = len(shape)
    return pl.BlockSpec((None,) + tuple(shape), lambda b, t: (layer,) + (0,) * nd,
                        pipeline_mode=pl.Buffered(1))


def _params():
    return pltpu.CompilerParams(dimension_semantics=("arbitrary", "arbitrary"),
                                vmem_limit_bytes=VMEM_LIMIT_V7X)


def _mix_call(cfg, x, eg, eb, win, wo, lb, hgn, rtg, rtb, l1g, l1b, cos, sin, hg0, rt0):
    C, L, Lv, nb, nt = cfg.C, cfg.L, cfg.Lv, cfg.nb, cfg.nt
    gs = C // L
    a_stack, lv, dec, qdec, kdec, g_total = _tables(C, L, Lv)
    a_stack = jnp.asarray(a_stack, BF16)
    l = cfg.layer
    rows = nb * nt * C
    row_spec = pl.BlockSpec((C, D_MODEL), lambda b, t: (b * nt + t, 0))
    if cfg.per_seq:
        st_spec = pl.BlockSpec((None, gs, HEADS, DK, DK), lambda b, t: (l, b, 0, 0, 0))
    else:
        st_spec = pl.BlockSpec((gs, HEADS, DK, DK), lambda b, t: (0, 0, 0, 0))
    in_specs = [
        row_spec,
        _const_spec((1, D_MODEL)), _const_spec((1, D_MODEL)),
        _layer_spec((D_MODEL, PROJ_COLS), l), _layer_spec((D_MODEL, D_MODEL), l),
        _layer_spec((1, HW), l), _layer_spec((1, HW), l), _layer_spec((1, HW), l), _layer_spec((1, HW), l),
        _layer_spec((1, D_MODEL), l), _layer_spec((1, D_MODEL), l),
        pl.BlockSpec((C, HW), lambda b, t: (t, 0)), pl.BlockSpec((C, HW), lambda b, t: (t, 0)),
        _const_spec(a_stack.shape), _const_spec(lv.shape), _const_spec(dec.shape),
        _const_spec(qdec.shape), _const_spec(kdec.shape),
        st_spec, st_spec,
    ]
    out_st = pl.BlockSpec((gs, HEADS, DK, DK), lambda b, t: (b, 0, 0, 0))
    out_shape = (jax.ShapeDtypeStruct((rows, D_MODEL), F32),
                 jax.ShapeDtypeStruct((nb * gs, HEADS, DK, DK), F32),
                 jax.ShapeDtypeStruct((nb * gs, HEADS, DK, DK), F32))
    return pl.pallas_call(
        functools.partial(_mix_kernel, cfg, g_total),
        out_shape=out_shape,
        grid=(nb, nt),
        in_specs=in_specs,
        out_specs=(row_spec, out_st, out_st),
        scratch_shapes=[pltpu.VMEM((HEADS, gs * DK, DK), F32), pltpu.VMEM((HEADS, gs * DK, DK), F32)],
        compiler_params=_params(),
        name=f"mixer_l{l}_L{L}_nt{nt}",
    )(x, eg, eb, win, wo, lb, hgn, rtg, rtb, l1g, l1b, cos, sin,
      a_stack, jnp.asarray(lv), jnp.asarray(dec), jnp.asarray(qdec), jnp.asarray(kdec), hg0, rt0)


def _ffn_call(cfg, x, wa, wb, cw, cb, wd, l2g, l2b, *conv_in):
    C, L, nb, nt = cfg.C, cfg.L, cfg.nb, cfg.nt
    gs = C // L
    l = cfg.layer
    rows = nb * nt * C
    row_spec = pl.BlockSpec((C, D_MODEL), lambda b, t: (b * nt + t, 0))
    in_specs = [
        row_spec,
        _layer_spec((D_MODEL, D_FF), l), _layer_spec((D_MODEL, D_FF), l),
        _layer_spec((CONV_W, D_FF), l), _layer_spec((1, D_FF), l),
        _layer_spec((D_FF, D_MODEL), l),
        _layer_spec((1, D_MODEL), l), _layer_spec((1, D_MODEL), l),
    ]
    if cfg.per_seq:
        fill_spec = pl.BlockSpec((C, D_FF), lambda b, t: (b, 0))
        in_specs += [fill_spec, fill_spec]
        cs_shape = jax.ShapeDtypeStruct((2, nb * gs, D_FF), F32)
        cs_spec = pl.BlockSpec((2, gs, D_FF), lambda b, t: (0, b, 0))
    else:
        in_specs += [pl.BlockSpec((1, CONV_W - 1, D_FF), lambda b, t: (0, 0, 0))]
        cs_shape = jax.ShapeDtypeStruct((nb, CONV_W - 1, D_FF), F32)
        cs_spec = pl.BlockSpec((1, CONV_W - 1, D_FF), lambda b, t: (b, 0, 0))
    return pl.pallas_call(
        functools.partial(_ffn_kernel, cfg),
        out_shape=(jax.ShapeDtypeStruct((rows, D_MODEL), F32), cs_shape),
        grid=(nb, nt),
        in_specs=in_specs,
        out_specs=(row_spec, cs_spec),
        scratch_shapes=[pltpu.VMEM((CARRY + C, D_FF), F32)],
        compiler_params=_params(),
        name=f"convffn_l{l}_L{L}_nt{nt}",
    )(x, wa, wb, cw, cb, wd, l2g, l2b, *conv_in)


def _rope_tables(pos):
    half = DK // 2
    inv = ROPE_BASE ** (-jnp.arange(half, dtype=F32) / half)
    ang = pos.astype(F32)[:, None] * inv[None, :]
    cos, sin = jnp.cos(ang), jnp.sin(ang)
    cos2 = jnp.concatenate([cos, cos], axis=1)
    sin2 = jnp.concatenate([-sin, sin], axis=1)
    return jnp.tile(cos2, (1, HEADS)), jnp.tile(sin2, (1, HEADS))


def kernel(x_prompt, x_sample, state_hgrn, state_ret, state_conv, meta_tokens, emb_ln_g, emb_ln_b, lb_logits, w_in, hg_norm_g, rt_gn_g, rt_gn_b, w_o, ln1_g, ln1_b, w_a, w_b, conv_w, conv_b, w_down, ln2_g, ln2_b):
    bsz, seq, _ = x_prompt.shape
    dec_b, dec_t, _ = x_sample.shape
    C = TILE
    assert seq % C == 0 and N_META <= C and dec_t >= CONV_W - 1
    L_s = 8
    assert dec_t <= L_s and (dec_b * L_s) % C == 0

    sm = jax.nn.softmax(lb_logits.astype(F32), axis=0)
    lb_all = (jnp.cumsum(sm, axis=0) - sm[0:1])[:, None, :]
    win, wo = w_in.astype(BF16), w_o.astype(BF16)
    wa, wb, wd = w_a.astype(BF16), w_b.astype(BF16), w_down.astype(BF16)
    vec = lambda v: v[:, None, :]
    hgn, rtg, rtb = vec(hg_norm_g), vec(rt_gn_g), vec(rt_gn_b)
    l1g, l1b, l2g, l2b, cb = vec(ln1_g), vec(ln1_b), vec(ln2_g), vec(ln2_b), vec(conv_b)
    eg, eb = emb_ln_g[None, :], emb_ln_b[None, :]

    cos_m, sin_m = _rope_tables(jnp.arange(C, dtype=jnp.int32))
    cos_p, sin_p = _rope_tables(N_META + jnp.arange(seq, dtype=jnp.int32))
    cos_s, sin_s = _rope_tables(PAST_LEN + jnp.arange(C, dtype=jnp.int32) % L_s)

    xm = jnp.pad(meta_tokens.astype(F32), ((0, C - N_META), (0, 0)))
    xp = x_prompt.reshape(bsz * seq, D_MODEL)
    xs = jnp.pad(x_sample, ((0, 0), (0, L_s - dec_t), (0, 0))).reshape(dec_b * L_s, D_MODEL)
    zero_state = jnp.zeros((1, HEADS, DK, DK), F32)
    zero_conv = jnp.zeros((1, CONV_W - 1, D_FF), F32)

    hg_p, hg_s, rt_p, rt_s, cv_p, cv_s = [], [], [], [], [], []
    for l in range(DEPTH):
        emb = l == 0
        cfg_m = Cfg(1, 1, C, C, N_META, emb, False, l)
        cfg_p = Cfg(bsz, seq // C, C, C, C, emb, False, l)
        cfg_s = Cfg(dec_b * L_s // C, 1, C, L_s, dec_t, emb, True, l)
        mix_w = (eg, eb, win, wo, lb_all, hgn, rtg, rtb, l1g, l1b)
        ffn_w = (wa, wb, conv_w, cb, wd, l2g, l2b)

        xm1, hg_m, rt_m = _mix_call(cfg_m, xm, *mix_w, cos_m, sin_m, zero_state, zero_state)
        xm, cv_m = _ffn_call(cfg_m, xm1, *ffn_w, zero_conv)

        xp1, hg, rt = _mix_call(cfg_p, xp, *mix_w, cos_p, sin_p, hg_m, rt_m)
        xp, cv = _ffn_call(cfg_p, xp1, *ffn_w, cv_m)
        hg_p.append(hg); rt_p.append(rt); cv_p.append(cv)

        xs1, hg, rt = _mix_call(cfg_s, xs, *mix_w, cos_s, sin_s, state_hgrn, state_ret)
        sc = state_conv[l]
        fill1 = jnp.pad(sc[:, 1:2], ((0, 0), (0, L_s - 1), (0, 0))).reshape(dec_b * L_s, D_FF)
        fill2 = jnp.pad(sc, ((0, 0), (0, L_s - 2), (0, 0))).reshape(dec_b * L_s, D_FF)
        xs, cv = _ffn_call(cfg_s, xs1, *ffn_w, fill1, fill2)
        hg_s.append(hg); rt_s.append(rt); cv_s.append(jnp.swapaxes(cv, 0, 1))

    y_prompt = xp.reshape(bsz, seq, D_MODEL)
    y_sample = xs.reshape(dec_b, L_s, D_MODEL)[:, :dec_t]
    return (y_prompt, y_sample, jnp.stack(hg_p), jnp.stack(hg_s), jnp.stack(rt_p), jnp.stack(rt_s),
            jnp.stack(cv_p), jnp.stack(cv_s))
```

```python
import functools
import math
from typing import NamedTuple

import numpy as np
import jax
import jax.numpy as jnp
from jax import lax
from jax.experimental import pallas as pl
from jax.experimental.pallas import tpu as pltpu

D_MODEL = 1024
DEPTH = 4
HEADS = 4
DK = 128
HW = HEADS * DK
PROJ_COLS = 8 * HW
N_META = 16
PAST_LEN = 16384
CONV_W = 3
D_FF = 2816
ROPE_BASE = 10000.0
LN_EPS = 1e-5
F_EPS = 1e-6
ALPHA = (2 * DEPTH) ** 0.25

TILE = 128
FFN_TILE = 256
FF_CHUNK = 256
CARRY = 8
VMEM_LIMIT_V7X = 56 * 1024 * 1024

F32 = jnp.float32
BF16 = jnp.bfloat16
NT_DIMS = (((1,), (1,)), ((), ()))
TN_DIMS = (((0,), (0,)), ((), ()))


class Cfg(NamedTuple):
    nb: int
    nt: int
    C: int
    L: int
    Lv: int
    embed: bool
    per_seq: bool
    layer: int


def _levels(Lv):
    return [m for m in (1, 2, 4, 8, 16, 32, 64, 128, 256) if m < Lv]


@functools.lru_cache(maxsize=None)
def _tables(C, L, Lv):
    r = np.arange(C)
    g, p = r // L, r % L
    same = g[:, None] == g[None, :]
    pt, pu = p[:, None], p[None, :]
    blocks = [same & (pu <= pt), same & (pu > pt)]
    lv = np.full((C, C), -1, np.int32)
    ok = same & (pt < Lv) & (pu < Lv)
    lv[ok & (pt == pu)] = 0
    for j, m in enumerate(_levels(Lv)):
        blk_t, blk_u = pt // (2 * m), pu // (2 * m)
        half_t, half_u = (pt // m) % 2, (pu // m) % 2
        inblk = same & (blk_t == blk_u)
        blocks.append(inblk & (((half_t == 1) & (half_u == 1) & (pu <= pt)) |
                               ((half_t == 0) & (half_u == 0) & (pu > pt))))
        lv[ok & inblk & (half_t == 1) & (half_u == 0)] = j + 1
    a_stack = np.concatenate(blocks, 0).astype(np.float32)

    log_gamma = np.log1p(-np.exp2(-5.0 - np.arange(HEADS, dtype=np.float64)))
    rel = (pt - pu).astype(np.float64)
    causal = same & (pu <= pt) & (pu < Lv)
    dec = np.where(causal[None], np.exp(np.where(causal, rel, 0.0)[None] * log_gamma[:, None, None]), 0.0)
    qdec = np.exp((p + 1.0)[:, None] * log_gamma[None, :])
    kdec = np.where((p < Lv)[:, None], np.exp((Lv - 1.0 - p)[:, None] * log_gamma[None, :]), 0.0)
    qdec = np.repeat(qdec, DK, axis=1)
    kdec = np.repeat(kdec, DK, axis=1)
    g_total = tuple(float(x) for x in np.exp(Lv * log_gamma))
    return (a_stack, lv, dec.astype(np.float32), qdec.astype(np.float32), kdec.astype(np.float32), g_total)


def _ln(x, g, b):
    mu = jnp.mean(x, -1, keepdims=True)
    xc = x - mu
    var = jnp.mean(xc * xc, -1, keepdims=True)
    return xc * lax.rsqrt(var + LN_EPS) * g + b


def _silu(x):
    return x * jax.nn.sigmoid(x)


def _expand(x, gs, mask):
    if gs == 1:
        return x
    return jnp.where(mask, jnp.concatenate([x] * gs, axis=1), 0.0)


def _col_bcast(rows, gs):
    outs = []
    for g in range(gs):
        outs.append(jnp.broadcast_to(rows[g:g + 1, :], (DK, DK)).T)
    return outs[0] if gs == 1 else jnp.concatenate(outs, axis=0)


def _mix_kernel(cfg, g_total, x_ref, eg_ref, eb_ref, win_ref, wo_ref, lb_ref, hgn_ref, rtg_ref, rtb_ref,
                l1g_ref, l1b_ref, cos_ref, sin_ref, a_ref, lv_ref, dec_ref, qdec_ref, kdec_ref,
                hg0_ref, rt0_ref, x1_ref, hgn_out, rtn_out, hg_s, rt_s):
    C, L, Lv = cfg.C, cfg.L, cfg.Lv
    gs = C // L
    levels = _levels(Lv)
    t = pl.program_id(1)

    @pl.when(t == 0)
    def _():
        for h in range(HEADS):
            hg_s[h] = hg0_ref[:, h].reshape(gs * DK, DK)
            rt_s[h] = rt0_ref[:, h].reshape(gs * DK, DK)

    x = x_ref[...]
    if cfg.embed:
        x = _ln(x, eg_ref[...], eb_ref[...])
    proj = jnp.dot(x.astype(BF16), win_ref[...], preferred_element_type=F32)

    if gs > 1:
        row_seq = lax.broadcasted_iota(jnp.int32, (C, gs * DK), 0) // L
        lane_blk = lax.broadcasted_iota(jnp.int32, (C, gs * DK), 1) // DK
        seq_mask = row_seq == lane_blk
    else:
        seq_mask = None
    lv = lv_ref[...]

    hq, hf = proj[:, 0:HW], proj[:, HW:2 * HW]
    hv, hgate = proj[:, 2 * HW:3 * HW], proj[:, 3 * HW:4 * HW]
    lb = lb_ref[...]
    f_gate = lb + (1.0 - lb) * jax.nn.sigmoid(hf)
    lf = jnp.log(jnp.maximum(f_gate, F_EPS))
    kx = (1.0 - lb) * jax.nn.sigmoid(-hf)
    q = _silu(hq)
    if Lv < L:
        real = (lax.broadcasted_iota(jnp.int32, (C, 1), 0) % L) < Lv
        lf = jnp.where(real, lf, 0.0)
        kx = jnp.where(real, kx, 0.0)
    a_mat = a_ref[...]
    lf_hi = lf.astype(BF16)
    rem = lf - lf_hi.astype(F32)
    lf_mid = rem.astype(BF16)
    lf_lo = (rem - lf_mid.astype(F32)).astype(BF16)
    args = (jnp.dot(a_mat, lf_hi, preferred_element_type=F32)
            + jnp.dot(a_mat, lf_mid, preferred_element_type=F32)
            + jnp.dot(a_mat, lf_lo, preferred_element_type=F32))
    b_incl = args[0:C]
    q_in = q * jnp.exp(b_incl)
    k_out = kx * jnp.exp(args[C:2 * C])
    if gs == 1:
        b_last = b_incl[Lv - 1:Lv, :]
    else:
        b_last = b_incl.reshape(gs, L, HW)[:, Lv - 1, :]
    d_rows = jnp.exp(b_last)

    outs = []
    for h in range(HEADS):
        sl = slice(h * DK, (h + 1) * DK)
        qh, kh = q[:, sl], kx[:, sl]
        vb = hv[:, sl].astype(BF16)
        sc = lax.dot_general(qh.astype(BF16), kh.astype(BF16), NT_DIMS, preferred_element_type=F32)
        acc = jnp.where(lv == 0, sc, 0.0)
        for j in range(len(levels)):
            e = jnp.exp(args[(2 + j) * C:(3 + j) * C, sl])
            sc = lax.dot_general((qh * e).astype(BF16), (kh * e).astype(BF16), NT_DIMS,
                                 preferred_element_type=F32)
            acc = jnp.where(lv == j + 1, sc, acc)
        s_old = hg_s[h]
        o = jnp.dot(acc.astype(BF16), vb, preferred_element_type=F32)
        o = o + jnp.dot(_expand(q_in[:, sl], gs, seq_mask).astype(BF16), s_old.astype(BF16),
                        preferred_element_type=F32)
        upd = lax.dot_general(_expand(k_out[:, sl], gs, seq_mask).astype(BF16), vb, TN_DIMS,
                              preferred_element_type=F32)
        hg_s[h] = _col_bcast(d_rows[:, sl], gs) * s_old + upd
        o = o * lax.rsqrt(jnp.mean(o * o, -1, keepdims=True) + LN_EPS)
        outs.append(o * hgn_ref[:, sl] * _silu(hgate[:, sl]))

    rq, rk = proj[:, 4 * HW:5 * HW], proj[:, 5 * HW:6 * HW]
    rv, rgate = proj[:, 6 * HW:7 * HW], proj[:, 7 * HW:8 * HW]
    cos, sin = cos_ref[...], sin_ref[...]
    qdec, kdec = qdec_ref[...], kdec_ref[...]
    for h in range(HEADS):
        sl = slice(h * DK, (h + 1) * DK)
        qh = rq[:, sl] * cos[:, sl] + pltpu.roll(rq[:, sl], DK // 2, 1) * sin[:, sl]
        kh = (rk[:, sl] * cos[:, sl] + pltpu.roll(rk[:, sl], DK // 2, 1) * sin[:, sl]) * (DK ** -0.5)
        vb = rv[:, sl].astype(BF16)
        sc = lax.dot_general(qh.astype(BF16), kh.astype(BF16), NT_DIMS, preferred_element_type=F32)
        sc = sc * dec_ref[h]
        s_old = rt_s[h]
        o = jnp.dot(sc.astype(BF16), vb, preferred_element_type=F32)
        o = o + jnp.dot(_expand(qh * qdec[:, sl], gs, seq_mask).astype(BF16), s_old.astype(BF16),
                        preferred_element_type=F32)
        upd = lax.dot_general(_expand(kh * kdec[:, sl], gs, seq_mask).astype(BF16), vb, TN_DIMS,
                              preferred_element_type=F32)
        rt_s[h] = g_total[h] * s_old + upd
        mu = jnp.mean(o, -1, keepdims=True)
        oc = o - mu
        o = oc * lax.rsqrt(jnp.mean(oc * oc, -1, keepdims=True) + LN_EPS)
        outs.append((o * rtg_ref[:, sl] + rtb_ref[:, sl]) * _silu(rgate[:, sl]))

    o_all = jnp.concatenate(outs, axis=1).astype(BF16)
    mix = jnp.dot(o_all, wo_ref[...], preferred_element_type=F32)
    x1_ref[...] = _ln(ALPHA * x + mix, l1g_ref[...], l1b_ref[...])

    @pl.when(t == cfg.nt - 1)
    def _():
        for h in range(HEADS):
            hgn_out[:, h] = hg_s[h].reshape(gs, DK, DK)
            rtn_out[:, h] = rt_s[h].reshape(gs, DK, DK)


def _ffn_kernel(cfg, x_ref, wa_ref, wb_ref, cw_ref, cb_ref, wd_ref, l2g_ref, l2b_ref, *rest):
    C, L, Lv = cfg.C, cfg.L, cfg.Lv
    gs = C // L
    if cfg.per_seq:
        fill1_ref, fill2_ref, y_ref, cs_ref, a_scr = rest
    else:
        init_ref, y_ref, cs_ref, a_scr = rest
    t = pl.program_id(1)

    x = x_ref[...]
    xb = x.astype(BF16)
    if cfg.per_seq:
        pos = lax.broadcasted_iota(jnp.int32, (C, 1), 0) % L
    else:
        @pl.when(t == 0)
        def _():
            a_scr[...] = jnp.zeros((CARRY, D_FF), F32)
            a_scr[CARRY - 2:CARRY, :] = init_ref[0]
        row8 = lax.broadcasted_iota(jnp.int32, (CARRY, 1), 0)

    def up_proj(j):
        cs = slice(j * FF_CHUNK, (j + 1) * FF_CHUNK)
        return (jnp.dot(xb, wa_ref[:, cs], preferred_element_type=F32),
                jnp.dot(xb, wb_ref[:, cs], preferred_element_type=F32))

    n_chunks = D_FF // FF_CHUNK
    ffn = jnp.zeros((C, D_MODEL), F32)
    ahead = up_proj(0)
    for j in range(n_chunks):
        cs = slice(j * FF_CHUNK, (j + 1) * FF_CHUNK)
        a, gate = ahead
        if j + 1 < n_chunks:
            ahead = up_proj(j + 1)
        a_m1 = pltpu.roll(a, 1, 0)
        a_m2 = pltpu.roll(a, 2, 0)
        if cfg.per_seq:
            a_m1 = jnp.where(pos >= 1, a_m1, fill1_ref[:, cs])
            a_m2 = jnp.where(pos >= 2, a_m2, fill2_ref[:, cs])
        else:
            prev = a_scr[:, cs]
            head1 = jnp.where(row8 == 0, prev[CARRY - 1:CARRY, :], a_m1[0:CARRY, :])
            head2 = jnp.where(row8 == 0, prev[CARRY - 2:CARRY - 1, :],
                              jnp.where(row8 == 1, prev[CARRY - 1:CARRY, :], a_m2[0:CARRY, :]))
            a_m1 = jnp.concatenate([head1, a_m1[CARRY:, :]], axis=0)
            a_m2 = jnp.concatenate([head2, a_m2[CARRY:, :]], axis=0)
            a_scr[:, cs] = a[C - CARRY:, :]
        conv = cb_ref[:, cs] + cw_ref[0:1, cs] * a_m2 + cw_ref[1:2, cs] * a_m1 + cw_ref[2:3, cs] * a
        hidden = (jax.nn.gelu(conv, approximate=True) * gate).astype(BF16)
        ffn = ffn + jnp.dot(hidden, wd_ref[cs, :], preferred_element_type=F32)
        if cfg.per_seq:
            a_seq = a.reshape(gs, L, FF_CHUNK)
            cs_ref[0, :, cs] = a_seq[:, Lv - 2, :]
            cs_ref[1, :, cs] = a_seq[:, Lv - 1, :]
        else:
            cs_ref[0, :, cs] = a[Lv - 2:Lv, :]
    y_ref[...] = _ln(ALPHA * x + ffn, l2g_ref[...], l2b_ref[...])


def _const_spec(shape):
    nd = len(shape)
    return pl.BlockSpec(shape, lambda b, t: (0,) * nd, pipeline_mode=pl.Buffered(1))


def _layer_spec(shape, layer):
    nd = len(shape)
    return pl.BlockSpec((None,) + tuple(shape), lambda b, t: (layer,) + (0,) * nd,
                        pipeline_mode=pl.Buffered(1))


def _params():
    return pltpu.CompilerParams(dimension_semantics=("arbitrary", "arbitrary"),
                                vmem_limit_bytes=VMEM_LIMIT_V7X)


def _mix_call(cfg, x, eg, eb, win, wo, lb, hgn, rtg, rtb, l1g, l1b, cos, sin, hg0, rt0):
    C, L, Lv, nb, nt = cfg.C, cfg.L, cfg.Lv, cfg.nb, cfg.nt
    gs = C // L
    a_stack, lv, dec, qdec, kdec, g_total = _tables(C, L, Lv)
    a_stack = jnp.asarray(a_stack, BF16)
    l = cfg.layer
    rows = nb * nt * C
    row_spec = pl.BlockSpec((C, D_MODEL), lambda b, t: (b * nt + t, 0))
    if cfg.per_seq:
        st_spec = pl.BlockSpec((None, gs, HEADS, DK, DK), lambda b, t: (l, b, 0, 0, 0))
    else:
        st_spec = pl.BlockSpec((gs, HEADS, DK, DK), lambda b, t: (0, 0, 0, 0))
    in_specs = [
        row_spec,
        _const_spec((1, D_MODEL)), _const_spec((1, D_MODEL)),
        _layer_spec((D_MODEL, PROJ_COLS), l), _layer_spec((D_MODEL, D_MODEL), l),
        _layer_spec((1, HW), l), _layer_spec((1, HW), l), _layer_spec((1, HW), l), _layer_spec((1, HW), l),
        _layer_spec((1, D_MODEL), l), _layer_spec((1, D_MODEL), l),
        pl.BlockSpec((C, HW), lambda b, t: (t, 0)), pl.BlockSpec((C, HW), lambda b, t: (t, 0)),
        _const_spec(a_stack.shape), _const_spec(lv.shape), _const_spec(dec.shape),
        _const_spec(qdec.shape), _const_spec(kdec.shape),
        st_spec, st_spec,
    ]
    out_st = pl.BlockSpec((gs, HEADS, DK, DK), lambda b, t: (b, 0, 0, 0))
    out_shape = (jax.ShapeDtypeStruct((rows, D_MODEL), F32),
                 jax.ShapeDtypeStruct((nb * gs, HEADS, DK, DK), F32),
                 jax.ShapeDtypeStruct((nb * gs, HEADS, DK, DK), F32))
    return pl.pallas_call(
        functools.partial(_mix_kernel, cfg, g_total),
        out_shape=out_shape,
        grid=(nb, nt),
        in_specs=in_specs,
        out_specs=(row_spec, out_st, out_st),
        scratch_shapes=[pltpu.VMEM((HEADS, gs * DK, DK), F32), pltpu.VMEM((HEADS, gs * DK, DK), F32)],
        compiler_params=_params(),
        name=f"mixer_l{l}_L{L}_nt{nt}",
    )(x, eg, eb, win, wo, lb, hgn, rtg, rtb, l1g, l1b, cos, sin,
      a_stack, jnp.asarray(lv), jnp.asarray(dec), jnp.asarray(qdec), jnp.asarray(kdec), hg0, rt0)


def _ffn_call(cfg, x, wa, wb, cw, cb, wd, l2g, l2b, *conv_in):
    C, L, nb, nt = cfg.C, cfg.L, cfg.nb, cfg.nt
    gs = C // L
    l = cfg.layer
    rows = nb * nt * C
    row_spec = pl.BlockSpec((C, D_MODEL), lambda b, t: (b * nt + t, 0))
    in_specs = [
        row_spec,
        _layer_spec((D_MODEL, D_FF), l), _layer_spec((D_MODEL, D_FF), l),
        _layer_spec((CONV_W, D_FF), l), _layer_spec((1, D_FF), l),
        _layer_spec((D_FF, D_MODEL), l),
        _layer_spec((1, D_MODEL), l), _layer_spec((1, D_MODEL), l),
    ]
    if cfg.per_seq:
        fill_spec = pl.BlockSpec((C, D_FF), lambda b, t: (b, 0))
        in_specs += [fill_spec, fill_spec]
        cs_shape = jax.ShapeDtypeStruct((2, nb * gs, D_FF), F32)
        cs_spec = pl.BlockSpec((2, gs, D_FF), lambda b, t: (0, b, 0))
    else:
        in_specs += [pl.BlockSpec((1, CONV_W - 1, D_FF), lambda b, t: (0, 0, 0))]
        cs_shape = jax.ShapeDtypeStruct((nb, CONV_W - 1, D_FF), F32)
        cs_spec = pl.BlockSpec((1, CONV_W - 1, D_FF), lambda b, t: (b, 0, 0))
    return pl.pallas_call(
        functools.partial(_ffn_kernel, cfg),
        out_shape=(jax.ShapeDtypeStruct((rows, D_MODEL), F32), cs_shape),
        grid=(nb, nt),
        in_specs=in_specs,
        out_specs=(row_spec, cs_spec),
        scratch_shapes=[pltpu.VMEM((CARRY, D_FF), F32)],
        compiler_params=_params(),
        name=f"convffn_l{l}_L{L}_nt{nt}",
    )(x, wa, wb, cw, cb, wd, l2g, l2b, *conv_in)


def _rope_tables(pos):
    half = DK // 2
    inv = ROPE_BASE ** (-jnp.arange(half, dtype=F32) / half)
    ang = pos.astype(F32)[:, None] * inv[None, :]
    cos, sin = jnp.cos(ang), jnp.sin(ang)
    cos2 = jnp.concatenate([cos, cos], axis=1)
    sin2 = jnp.concatenate([-sin, sin], axis=1)
    return jnp.tile(cos2, (1, HEADS)), jnp.tile(sin2, (1, HEADS))


def kernel(x_prompt, x_sample, state_hgrn, state_ret, state_conv, meta_tokens, emb_ln_g, emb_ln_b, lb_logits, w_in, hg_norm_g, rt_gn_g, rt_gn_b, w_o, ln1_g, ln1_b, w_a, w_b, conv_w, conv_b, w_down, ln2_g, ln2_b):
    bsz, seq, _ = x_prompt.shape
    dec_b, dec_t, _ = x_sample.shape
    C = TILE
    assert seq % C == 0 and N_META <= C and dec_t >= CONV_W - 1
    L_s = 8
    assert dec_t <= L_s and (dec_b * L_s) % C == 0

    sm = jax.nn.softmax(lb_logits.astype(F32), axis=0)
    lb_all = (jnp.cumsum(sm, axis=0) - sm[0:1])[:, None, :]
    win, wo = w_in.astype(BF16), w_o.astype(BF16)
    wa, wb, wd = w_a.astype(BF16), w_b.astype(BF16), w_down.astype(BF16)
    vec = lambda v: v[:, None, :]
    hgn, rtg, rtb = vec(hg_norm_g), vec(rt_gn_g), vec(rt_gn_b)
    l1g, l1b, l2g, l2b, cb = vec(ln1_g), vec(ln1_b), vec(ln2_g), vec(ln2_b), vec(conv_b)
    eg, eb = emb_ln_g[None, :], emb_ln_b[None, :]

    cos_m, sin_m = _rope_tables(jnp.arange(C, dtype=jnp.int32))
    cos_p, sin_p = _rope_tables(N_META + jnp.arange(seq, dtype=jnp.int32))
    cos_s, sin_s = _rope_tables(PAST_LEN + jnp.arange(C, dtype=jnp.int32) % L_s)

    xm = jnp.pad(meta_tokens.astype(F32), ((0, C - N_META), (0, 0)))
    xp = x_prompt.reshape(bsz * seq, D_MODEL)
    xs = jnp.pad(x_sample, ((0, 0), (0, L_s - dec_t), (0, 0))).reshape(dec_b * L_s, D_MODEL)
    zero_state = jnp.zeros((1, HEADS, DK, DK), F32)
    zero_conv = jnp.zeros((1, CONV_W - 1, D_FF), F32)

    hg_p, hg_s, rt_p, rt_s, cv_p, cv_s = [], [], [], [], [], []
    for l in range(DEPTH):
        emb = l == 0
        cfg_m = Cfg(1, 1, C, C, N_META, emb, False, l)
        cfg_p = Cfg(bsz, seq // C, C, C, C, emb, False, l)
        cfg_s = Cfg(dec_b * L_s // C, 1, C, L_s, dec_t, emb, True, l)
        mix_w = (eg, eb, win, wo, lb_all, hgn, rtg, rtb, l1g, l1b)
        ffn_w = (wa, wb, conv_w, cb, wd, l2g, l2b)

        xm1, hg_m, rt_m = _mix_call(cfg_m, xm, *mix_w, cos_m, sin_m, zero_state, zero_state)
        xm, cv_m = _ffn_call(cfg_m, xm1, *ffn_w, zero_conv)

        xp1, hg, rt = _mix_call(cfg_p, xp, *mix_w, cos_p, sin_p, hg_m, rt_m)
        cfg_pf = Cfg(bsz, seq // FFN_TILE, FFN_TILE, FFN_TILE, FFN_TILE, False, False, l)
        xp, cv = _ffn_call(cfg_pf, xp1, *ffn_w, cv_m)
        hg_p.append(hg); rt_p.append(rt); cv_p.append(cv)

        xs1, hg, rt = _mix_call(cfg_s, xs, *mix_w, cos_s, sin_s, state_hgrn, state_ret)
        sc = state_conv[l]
        fill1 = jnp.pad(sc[:, 1:2], ((0, 0), (0, L_s - 1), (0, 0))).reshape(dec_b * L_s, D_FF)
        fill2 = jnp.pad(sc, ((0, 0), (0, L_s - 2), (0, 0))).reshape(dec_b * L_s, D_FF)
        xs, cv = _ffn_call(cfg_s, xs1, *ffn_w, fill1, fill2)
        hg_s.append(hg); rt_s.append(rt); cv_s.append(jnp.swapaxes(cv, 0, 1))

    y_prompt = xp.reshape(bsz, seq, D_MODEL)
    y_sample = xs.reshape(dec_b, L_s, D_MODEL)[:, :dec_t]
    return (y_prompt, y_sample, jnp.stack(hg_p), jnp.stack(hg_s), jnp.stack(rt_p), jnp.stack(rt_s),
            jnp.stack(cv_p), jnp.stack(cv_s))
```

```python
import functools
from typing import NamedTuple

import numpy as np
import jax
import jax.numpy as jnp
from jax import lax
from jax.experimental import pallas as pl
from jax.experimental.pallas import tpu as pltpu

D_MODEL = 1024
DEPTH = 4
HEADS = 4
DK = 128
HW = HEADS * DK
PROJ_COLS = 8 * HW
N_META = 16
PAST_LEN = 16384
CONV_W = 3
D_FF = 2816
ROPE_BASE = 10000.0
LN_EPS = 1e-5
F_EPS = 1e-6
ALPHA = (2 * DEPTH) ** 0.25

TILE = 128
FFN_TILE = 256
FF_CHUNK = 256
CARRY = 8
VMEM_LIMIT_V7X = 56 * 1024 * 1024

F32 = jnp.float32
BF16 = jnp.bfloat16
NT_DIMS = (((1,), (1,)), ((), ()))
TN_DIMS = (((0,), (0,)), ((), ()))


class Cfg(NamedTuple):
    nb: int
    nt: int
    C: int
    L: int
    Lv: int
    embed: bool
    per_seq: bool
    layer: int


def _levels(Lv):
    return [m for m in (1, 2, 4, 8, 16, 32, 64, 128, 256) if m < Lv]


@functools.lru_cache(maxsize=None)
def _tables(C, L, Lv):
    r = np.arange(C)
    g, p = r // L, r % L
    same = g[:, None] == g[None, :]
    pt, pu = p[:, None], p[None, :]
    blocks = [same & (pu <= pt), same & (pu > pt)]
    lv = np.full((C, C), -1, np.int32)
    ok = same & (pt < Lv) & (pu < Lv)
    lv[ok & (pt == pu)] = 0
    for j, m in enumerate(_levels(Lv)):
        blk_t, blk_u = pt // (2 * m), pu // (2 * m)
        half_t, half_u = (pt // m) % 2, (pu // m) % 2
        lv[ok & same & (blk_t == blk_u) & (half_t == 1) & (half_u == 0)] = j + 1
    a_stack = np.concatenate(blocks, 0).astype(np.float32)

    log_gamma = np.log1p(-np.exp2(-5.0 - np.arange(HEADS, dtype=np.float64)))
    rel = (pt - pu).astype(np.float64)
    causal = same & (pu <= pt) & (pu < Lv)
    dec = np.where(causal[None], np.exp(np.where(causal, rel, 0.0)[None] * log_gamma[:, None, None]), 0.0)
    qdec = np.exp((p + 1.0)[:, None] * log_gamma[None, :])
    kdec = np.where((p < Lv)[:, None], np.exp((Lv - 1.0 - p)[:, None] * log_gamma[None, :]), 0.0)
    qdec = np.repeat(qdec, DK, axis=1)
    kdec = np.repeat(kdec, DK, axis=1)
    g_total = tuple(float(x) for x in np.exp(Lv * log_gamma))
    return (a_stack, lv, dec.astype(np.float32), qdec.astype(np.float32), kdec.astype(np.float32), g_total)


def _ln(x, g, b):
    mu = jnp.mean(x, -1, keepdims=True)
    xc = x - mu
    var = jnp.mean(xc * xc, -1, keepdims=True)
    return xc * lax.rsqrt(var + LN_EPS) * g + b


def _silu(x):
    return x * jax.nn.sigmoid(x)


def _expand(x, gs, mask):
    if gs == 1:
        return x
    return jnp.where(mask, jnp.concatenate([x] * gs, axis=1), 0.0)


def _col_bcast(rows, gs):
    outs = []
    for g in range(gs):
        outs.append(jnp.broadcast_to(rows[g:g + 1, :], (DK, DK)).T)
    return outs[0] if gs == 1 else jnp.concatenate(outs, axis=0)


def _boundary_rows(b, m):
    C, W = b.shape
    blk = 2 * m
    if blk >= 8:
        b3 = b.reshape(C // blk, blk, W)
        return jnp.broadcast_to(b3[:, m - 1:m, :], b3.shape).reshape(C, W)
    b3 = b.reshape(C // 8, 8, W)
    sub = lax.broadcasted_iota(jnp.int32, (1, 8, 1), 1)
    r = jnp.broadcast_to(b3[:, m - 1:m, :], b3.shape)
    for k in range(1, 8 // blk):
        r = jnp.where(sub >= k * blk, jnp.broadcast_to(b3[:, k * blk + m - 1:k * blk + m, :], b3.shape), r)
    return r.reshape(C, W)


def _mix_kernel(cfg, g_total, x_ref, eg_ref, eb_ref, win_ref, wo_ref, lb_ref, hgn_ref, rtg_ref, rtb_ref,
                l1g_ref, l1b_ref, cos_ref, sin_ref, a_ref, lv_ref, dec_ref, qdec_ref, kdec_ref,
                hg0_ref, rt0_ref, x1_ref, hg_ref, rt_ref):
    C, L, Lv = cfg.C, cfg.L, cfg.Lv
    gs = C // L
    levels = _levels(Lv)
    t = pl.program_id(1)
    heads = [slice(h * DK, (h + 1) * DK) for h in range(HEADS)]

    @pl.when(t == 0)
    def _():
        hg_ref[...] = hg0_ref[...]
        rt_ref[...] = rt0_ref[...]

    x = x_ref[...]
    if cfg.embed:
        x = _ln(x, eg_ref[...], eb_ref[...])
    proj = jnp.dot(x.astype(BF16), win_ref[...], preferred_element_type=F32)

    if gs > 1:
        row_seq = lax.broadcasted_iota(jnp.int32, (C, gs * DK), 0) // L
        lane_blk = lax.broadcasted_iota(jnp.int32, (C, gs * DK), 1) // DK
        seq_mask = row_seq == lane_blk
    else:
        seq_mask = None
    lv = lv_ref[...]


    rq, rk = proj[:, 4 * HW:5 * HW], proj[:, 5 * HW:6 * HW]
    rv, rgate = proj[:, 6 * HW:7 * HW], proj[:, 7 * HW:8 * HW]
    cos, sin = cos_ref[...], sin_ref[...]
    qdec, kdec = qdec_ref[...], kdec_ref[...]
    rt_q, rt_k, rt_sc = [], [], []
    for h, sl in enumerate(heads):
        qh = rq[:, sl] * cos[:, sl] + pltpu.roll(rq[:, sl], DK // 2, 1) * sin[:, sl]
        kh = (rk[:, sl] * cos[:, sl] + pltpu.roll(rk[:, sl], DK // 2, 1) * sin[:, sl]) * (DK ** -0.5)
        sc = lax.dot_general(qh.astype(BF16), kh.astype(BF16), NT_DIMS, preferred_element_type=F32)
        rt_q.append(qh)
        rt_k.append(kh)
        rt_sc.append((sc * dec_ref[h]).astype(BF16))

    hq, hf = proj[:, 0:HW], proj[:, HW:2 * HW]
    hv, hgate = proj[:, 2 * HW:3 * HW], proj[:, 3 * HW:4 * HW]
    lb = lb_ref[...]
    f_gate = lb + (1.0 - lb) * jax.nn.sigmoid(hf)
    lf = jnp.log(jnp.maximum(f_gate, F_EPS))
    kx = (1.0 - lb) * jax.nn.sigmoid(-hf)
    q = _silu(hq)
    if Lv < L:
        real = (lax.broadcasted_iota(jnp.int32, (C, 1), 0) % L) < Lv
        lf = jnp.where(real, lf, 0.0)
        kx = jnp.where(real, kx, 0.0)
    a_mat = a_ref[...]
    lf_hi = lf.astype(BF16)
    rem = lf - lf_hi.astype(F32)
    lf_mid = rem.astype(BF16)
    lf_lo = (rem - lf_mid.astype(F32)).astype(BF16)
    sums = (jnp.dot(a_mat, lf_hi, preferred_element_type=F32)
            + jnp.dot(a_mat, lf_mid, preferred_element_type=F32)
            + jnp.dot(a_mat, lf_lo, preferred_element_type=F32))
    b_incl = sums[0:C]
    q_in = q * jnp.exp(b_incl)
    k_out = kx * jnp.exp(sums[C:2 * C])
    if gs == 1:
        b_last = b_incl[Lv - 1:Lv, :]
    else:
        b_last = b_incl.reshape(gs, L, HW)[:, Lv - 1, :]
    d_rows = jnp.exp(b_last)

    qb, kb = q.astype(BF16), kx.astype(BF16)
    acc = []
    for sl in heads:
        sc = lax.dot_general(qb[:, sl], kb[:, sl], NT_DIMS, preferred_element_type=F32)
        acc.append(jnp.where(lv == 0, sc, 0.0))
    for j, m in enumerate(levels):
        e = jnp.exp(-jnp.abs(b_incl - _boundary_rows(b_incl, m)))
        qe, ke = (q * e).astype(BF16), (kx * e).astype(BF16)
        for h, sl in enumerate(heads):
            sc = lax.dot_general(qe[:, sl], ke[:, sl], NT_DIMS, preferred_element_type=F32)
            acc[h] = jnp.where(lv == j + 1, sc, acc[h])

    outs_rt = []
    for h, sl in enumerate(heads):
        vb = rv[:, sl].astype(BF16)
        s_old = rt_ref[:, h].reshape(gs * DK, DK)
        o = jnp.dot(rt_sc[h], vb, preferred_element_type=F32)
        o = o + jnp.dot(_expand(rt_q[h] * qdec[:, sl], gs, seq_mask).astype(BF16), s_old.astype(BF16),
                        preferred_element_type=F32)
        upd = lax.dot_general(_expand(rt_k[h] * kdec[:, sl], gs, seq_mask).astype(BF16), vb, TN_DIMS,
                              preferred_element_type=F32)
        rt_ref[:, h] = (g_total[h] * s_old + upd).reshape(gs, DK, DK)
        mu = jnp.mean(o, -1, keepdims=True)
        oc = o - mu
        o = oc * lax.rsqrt(jnp.mean(oc * oc, -1, keepdims=True) + LN_EPS)
        outs_rt.append((o * rtg_ref[:, sl] + rtb_ref[:, sl]) * _silu(rgate[:, sl]))

    outs_hg = []
    for h, sl in enumerate(heads):
        vb = hv[:, sl].astype(BF16)
        s_old = hg_ref[:, h].reshape(gs * DK, DK)
        o = jnp.dot(acc[h].astype(BF16), vb, preferred_element_type=F32)
        o = o + jnp.dot(_expand(q_in[:, sl], gs, seq_mask).astype(BF16), s_old.astype(BF16),
                        preferred_element_type=F32)
        upd = lax.dot_general(_expand(k_out[:, sl], gs, seq_mask).astype(BF16), vb, TN_DIMS,
                              preferred_element_type=F32)
        hg_ref[:, h] = (_col_bcast(d_rows[:, sl], gs) * s_old + upd).reshape(gs, DK, DK)
        o = o * lax.rsqrt(jnp.mean(o * o, -1, keepdims=True) + LN_EPS)
        outs_hg.append(o * hgn_ref[:, sl] * _silu(hgate[:, sl]))

    o_all = jnp.concatenate(outs_hg + outs_rt, axis=1).astype(BF16)
    mix = jnp.dot(o_all, wo_ref[...], preferred_element_type=F32)
    x1_ref[...] = _ln(ALPHA * x + mix, l1g_ref[...], l1b_ref[...])


def _ffn_kernel(cfg, x_ref, wa_ref, wb_ref, cw_ref, cb_ref, wd_ref, l2g_ref, l2b_ref, *rest):
    C, L, Lv = cfg.C, cfg.L, cfg.Lv
    gs = C // L
    if cfg.per_seq:
        fill1_ref, fill2_ref, y_ref, cs_ref, a_scr = rest
    else:
        init_ref, y_ref, cs_ref, a_scr = rest
    t = pl.program_id(1)

    x = x_ref[...]
    xb = x.astype(BF16)
    if cfg.per_seq:
        pos = lax.broadcasted_iota(jnp.int32, (C, 1), 0) % L
    else:
        @pl.when(t == 0)
        def _():
            a_scr[...] = jnp.zeros((CARRY, D_FF), F32)
            a_scr[CARRY - 2:CARRY, :] = init_ref[0]
        row8 = lax.broadcasted_iota(jnp.int32, (CARRY, 1), 0)

    def up_proj(j):
        cs = slice(j * FF_CHUNK, (j + 1) * FF_CHUNK)
        return (jnp.dot(xb, wa_ref[:, cs], preferred_element_type=F32),
                jnp.dot(xb, wb_ref[:, cs], preferred_element_type=F32))

    n_chunks = D_FF // FF_CHUNK
    ffn = jnp.zeros((C, D_MODEL), F32)
    ahead = up_proj(0)
    for j in range(n_chunks):
        cs = slice(j * FF_CHUNK, (j + 1) * FF_CHUNK)
        a, gate = ahead
        if j + 1 < n_chunks:
            ahead = up_proj(j + 1)
        a_m1 = pltpu.roll(a, 1, 0)
        a_m2 = pltpu.roll(a, 2, 0)
        if cfg.per_seq:
            a_m1 = jnp.where(pos >= 1, a_m1, fill1_ref[:, cs])
            a_m2 = jnp.where(pos >= 2, a_m2, fill2_ref[:, cs])
        else:
            prev = a_scr[:, cs]
            head1 = jnp.where(row8 == 0, prev[CARRY - 1:CARRY, :], a_m1[0:CARRY, :])
            head2 = jnp.where(row8 == 0, prev[CARRY - 2:CARRY - 1, :],
                              jnp.where(row8 == 1, prev[CARRY - 1:CARRY, :], a_m2[0:CARRY, :]))
            a_m1 = jnp.concatenate([head1, a_m1[CARRY:, :]], axis=0)
            a_m2 = jnp.concatenate([head2, a_m2[CARRY:, :]], axis=0)
            a_scr[:, cs] = a[C - CARRY:, :]
        conv = cb_ref[:, cs] + cw_ref[0:1, cs] * a_m2 + cw_ref[1:2, cs] * a_m1 + cw_ref[2:3, cs] * a
        hidden = (jax.nn.gelu(conv, approximate=True) * gate).astype(BF16)
        ffn = ffn + jnp.dot(hidden, wd_ref[cs, :], preferred_element_type=F32)
        if cfg.per_seq:
            a_seq = a.reshape(gs, L, FF_CHUNK)
            cs_ref[0, :, cs] = a_seq[:, Lv - 2, :]
            cs_ref[1, :, cs] = a_seq[:, Lv - 1, :]
        else:
            cs_ref[0, :, cs] = a[Lv - 2:Lv, :]
    y_ref[...] = _ln(ALPHA * x + ffn, l2g_ref[...], l2b_ref[...])


def _const_spec(shape):
    nd = len(shape)
    return pl.BlockSpec(shape, lambda b, t: (0,) * nd, pipeline_mode=pl.Buffered(1))


def _layer_spec(shape, layer):
    nd = len(shape)
    return pl.BlockSpec((None,) + tuple(shape), lambda b, t: (layer,) + (0,) * nd,
                        pipeline_mode=pl.Buffered(1))


def _params():
    return pltpu.CompilerParams(dimension_semantics=("arbitrary", "arbitrary"),
                                vmem_limit_bytes=VMEM_LIMIT_V7X)


def _mix_call(cfg, x, eg, eb, win, wo, lb, hgn, rtg, rtb, l1g, l1b, cos, sin, hg0, rt0):
    C, L, Lv, nb, nt = cfg.C, cfg.L, cfg.Lv, cfg.nb, cfg.nt
    gs = C // L
    a_stack, lv, dec, qdec, kdec, g_total = _tables(C, L, Lv)
    a_stack = jnp.asarray(a_stack, BF16)
    l = cfg.layer
    rows = nb * nt * C
    row_spec = pl.BlockSpec((C, D_MODEL), lambda b, t: (b * nt + t, 0))
    if cfg.per_seq:
        st_spec = pl.BlockSpec((None, gs, HEADS, DK, DK), lambda b, t: (l, b, 0, 0, 0))
    else:
        st_spec = pl.BlockSpec((gs, HEADS, DK, DK), lambda b, t: (0, 0, 0, 0))
    in_specs = [
        row_spec,
        _const_spec((1, D_MODEL)), _const_spec((1, D_MODEL)),
        _layer_spec((D_MODEL, PROJ_COLS), l), _layer_spec((D_MODEL, D_MODEL), l),
        _layer_spec((1, HW), l), _layer_spec((1, HW), l), _layer_spec((1, HW), l), _layer_spec((1, HW), l),
        _layer_spec((1, D_MODEL), l), _layer_spec((1, D_MODEL), l),
        pl.BlockSpec((C, HW), lambda b, t: (t, 0)), pl.BlockSpec((C, HW), lambda b, t: (t, 0)),
        _const_spec(a_stack.shape), _const_spec(lv.shape), _const_spec(dec.shape),
        _const_spec(qdec.shape), _const_spec(kdec.shape),
        st_spec, st_spec,
    ]
    out_st = pl.BlockSpec((gs, HEADS, DK, DK), lambda b, t: (b, 0, 0, 0))
    out_shape = (jax.ShapeDtypeStruct((rows, D_MODEL), F32),
                 jax.ShapeDtypeStruct((nb * gs, HEADS, DK, DK), F32),
                 jax.ShapeDtypeStruct((nb * gs, HEADS, DK, DK), F32))
    return pl.pallas_call(
        functools.partial(_mix_kernel, cfg, g_total),
        out_shape=out_shape,
        grid=(nb, nt),
        in_specs=in_specs,
        out_specs=(row_spec, out_st, out_st),
        compiler_params=_params(),
        name=f"mixer_l{l}_L{L}_nt{nt}",
    )(x, eg, eb, win, wo, lb, hgn, rtg, rtb, l1g, l1b, cos, sin,
      a_stack, jnp.asarray(lv), jnp.asarray(dec), jnp.asarray(qdec), jnp.asarray(kdec), hg0, rt0)


def _ffn_call(cfg, x, wa, wb, cw, cb, wd, l2g, l2b, *conv_in):
    C, L, nb, nt = cfg.C, cfg.L, cfg.nb, cfg.nt
    gs = C // L
    l = cfg.layer
    rows = nb * nt * C
    row_spec = pl.BlockSpec((C, D_MODEL), lambda b, t: (b * nt + t, 0))
    in_specs = [
        row_spec,
        _layer_spec((D_MODEL, D_FF), l), _layer_spec((D_MODEL, D_FF), l),
        _layer_spec((CONV_W, D_FF), l), _layer_spec((1, D_FF), l),
        _layer_spec((D_FF, D_MODEL), l),
        _layer_spec((1, D_MODEL), l), _layer_spec((1, D_MODEL), l),
    ]
    if cfg.per_seq:
        fill_spec = pl.BlockSpec((C, D_FF), lambda b, t: (b, 0))
        in_specs += [fill_spec, fill_spec]
        cs_shape = jax.ShapeDtypeStruct((2, nb * gs, D_FF), F32)
        cs_spec = pl.BlockSpec((2, gs, D_FF), lambda b, t: (0, b, 0))
    else:
        in_specs += [pl.BlockSpec((1, CONV_W - 1, D_FF), lambda b, t: (0, 0, 0))]
        cs_shape = jax.ShapeDtypeStruct((nb, CONV_W - 1, D_FF), F32)
        cs_spec = pl.BlockSpec((1, CONV_W - 1, D_FF), lambda b, t: (b, 0, 0))
    return pl.pallas_call(
        functools.partial(_ffn_kernel, cfg),
        out_shape=(jax.ShapeDtypeStruct((rows, D_MODEL), F32), cs_shape),
        grid=(nb, nt),
        in_specs=in_specs,
        out_specs=(row_spec, cs_spec),
        scratch_shapes=[pltpu.VMEM((CARRY, D_FF), F32)],
        compiler_params=_params(),
        name=f"convffn_l{l}_L{L}_nt{nt}",
    )(x, wa, wb, cw, cb, wd, l2g, l2b, *conv_in)


def _rope_tables(pos):
    half = DK // 2
    inv = ROPE_BASE ** (-jnp.arange(half, dtype=F32) / half)
    ang = pos.astype(F32)[:, None] * inv[None, :]
    cos, sin = jnp.cos(ang), jnp.sin(ang)
    cos2 = jnp.concatenate([cos, cos], axis=1)
    sin2 = jnp.concatenate([-sin, sin], axis=1)
    return jnp.tile(cos2, (1, HEADS)), jnp.tile(sin2, (1, HEADS))


def kernel(x_prompt, x_sample, state_hgrn, state_ret, state_conv, meta_tokens, emb_ln_g, emb_ln_b, lb_logits, w_in, hg_norm_g, rt_gn_g, rt_gn_b, w_o, ln1_g, ln1_b, w_a, w_b, conv_w, conv_b, w_down, ln2_g, ln2_b):
    bsz, seq, _ = x_prompt.shape
    dec_b, dec_t, _ = x_sample.shape
    C = TILE
    assert seq % C == 0 and seq % FFN_TILE == 0 and N_META <= C and dec_t >= CONV_W - 1
    L_s = 8
    assert dec_t <= L_s and (dec_b * L_s) % C == 0

    sm = jax.nn.softmax(lb_logits.astype(F32), axis=0)
    lb_all = (jnp.cumsum(sm, axis=0) - sm[0:1])[:, None, :]
    win, wo = w_in.astype(BF16), w_o.astype(BF16)
    wa, wb, wd = w_a.astype(BF16), w_b.astype(BF16), w_down.astype(BF16)
    vec = lambda v: v[:, None, :]
    hgn, rtg, rtb = vec(hg_norm_g), vec(rt_gn_g), vec(rt_gn_b)
    l1g, l1b, l2g, l2b, cb = vec(ln1_g), vec(ln1_b), vec(ln2_g), vec(ln2_b), vec(conv_b)
    eg, eb = emb_ln_g[None, :], emb_ln_b[None, :]

    cos_m, sin_m = _rope_tables(jnp.arange(C, dtype=jnp.int32))
    cos_p, sin_p = _rope_tables(N_META + jnp.arange(seq, dtype=jnp.int32))
    cos_s, sin_s = _rope_tables(PAST_LEN + jnp.arange(C, dtype=jnp.int32) % L_s)

    xm = jnp.pad(meta_tokens.astype(F32), ((0, C - N_META), (0, 0)))
    xp = x_prompt.reshape(bsz * seq, D_MODEL)
    xs = jnp.pad(x_sample, ((0, 0), (0, L_s - dec_t), (0, 0))).reshape(dec_b * L_s, D_MODEL)
    zero_state = jnp.zeros((1, HEADS, DK, DK), F32)
    zero_conv = jnp.zeros((1, CONV_W - 1, D_FF), F32)

    hg_p, hg_s, rt_p, rt_s, cv_p, cv_s = [], [], [], [], [], []
    for l in range(DEPTH):
        emb = l == 0
        cfg_m = Cfg(1, 1, C, C, N_META, emb, False, l)
        cfg_p = Cfg(bsz, seq // C, C, C, C, emb, False, l)
        cfg_s = Cfg(dec_b * L_s // C, 1, C, L_s, dec_t, emb, True, l)
        mix_w = (eg, eb, win, wo, lb_all, hgn, rtg, rtb, l1g, l1b)
        ffn_w = (wa, wb, conv_w, cb, wd, l2g, l2b)

        xm1, hg_m, rt_m = _mix_call(cfg_m, xm, *mix_w, cos_m, sin_m, zero_state, zero_state)
        xm, cv_m = _ffn_call(cfg_m, xm1, *ffn_w, zero_conv)

        xp1, hg, rt = _mix_call(cfg_p, xp, *mix_w, cos_p, sin_p, hg_m, rt_m)
        cfg_pf = Cfg(bsz, seq // FFN_TILE, FFN_TILE, FFN_TILE, FFN_TILE, False, False, l)
        xp, cv = _ffn_call(cfg_pf, xp1, *ffn_w, cv_m)
        hg_p.append(hg)
        rt_p.append(rt)
        cv_p.append(cv)

        xs1, hg, rt = _mix_call(cfg_s, xs, *mix_w, cos_s, sin_s, state_hgrn, state_ret)
        sc = state_conv[l]
        fill1 = jnp.pad(sc[:, 1:2], ((0, 0), (0, L_s - 1), (0, 0))).reshape(dec_b * L_s, D_FF)
        fill2 = jnp.pad(sc, ((0, 0), (0, L_s - 2), (0, 0))).reshape(dec_b * L_s, D_FF)
        xs, cv = _ffn_call(cfg_s, xs1, *ffn_w, fill1, fill2)
        hg_s.append(hg)
        rt_s.append(rt)
        cv_s.append(jnp.swapaxes(cv, 0, 1))

    y_prompt = xp.reshape(bsz, seq, D_MODEL)
    y_sample = xs.reshape(dec_b, L_s, D_MODEL)[:, :dec_t]
    return (y_prompt, y_sample, jnp.stack(hg_p), jnp.stack(hg_s), jnp.stack(rt_p), jnp.stack(rt_s),
            jnp.stack(cv_p), jnp.stack(cv_s))
```

```python
import functools
from typing import NamedTuple

import numpy as np
import jax
import jax.numpy as jnp
from jax import lax
from jax.experimental import pallas as pl
from jax.experimental.pallas import tpu as pltpu

D_MODEL = 1024
DEPTH = 4
HEADS = 4
DK = 128
HW = HEADS * DK
PROJ_COLS = 8 * HW
N_META = 16
PAST_LEN = 16384
CONV_W = 3
D_FF = 2816
ROPE_BASE = 10000.0
LN_EPS = 1e-5
F_EPS = 1e-6
ALPHA = (2 * DEPTH) ** 0.25

TILE = 128
MIX_SUB = 4
FFN_TILE = 256
FF_CHUNK = 256
CARRY = 8
VMEM_LIMIT_V7X = 56 * 1024 * 1024

F32 = jnp.float32
BF16 = jnp.bfloat16
NT_DIMS = (((1,), (1,)), ((), ()))
TN_DIMS = (((0,), (0,)), ((), ()))


class Cfg(NamedTuple):
    nb: int
    nt: int
    C: int
    L: int
    Lv: int
    embed: bool
    per_seq: bool
    layer: int
    sub: int = 1


def _levels(Lv):
    return [m for m in (1, 2, 4, 8, 16, 32, 64, 128, 256) if m < Lv]


@functools.lru_cache(maxsize=None)
def _tables(C, L, Lv):
    r = np.arange(C)
    g, p = r // L, r % L
    same = g[:, None] == g[None, :]
    pt, pu = p[:, None], p[None, :]
    blocks = [same & (pu <= pt), same & (pu > pt)]
    lv = np.full((C, C), -1, np.int32)
    ok = same & (pt < Lv) & (pu < Lv)
    lv[ok & (pt == pu)] = 0
    for j, m in enumerate(_levels(Lv)):
        blk_t, blk_u = pt // (2 * m), pu // (2 * m)
        half_t, half_u = (pt // m) % 2, (pu // m) % 2
        lv[ok & same & (blk_t == blk_u) & (half_t == 1) & (half_u == 0)] = j + 1
    a_stack = np.concatenate(blocks, 0).astype(np.float32)

    log_gamma = np.log1p(-np.exp2(-5.0 - np.arange(HEADS, dtype=np.float64)))
    rel = (pt - pu).astype(np.float64)
    causal = same & (pu <= pt) & (pu < Lv)
    dec = np.where(causal[None], np.exp(np.where(causal, rel, 0.0)[None] * log_gamma[:, None, None]), 0.0)
    qdec = np.exp((p + 1.0)[:, None] * log_gamma[None, :])
    kdec = np.where((p < Lv)[:, None], np.exp((Lv - 1.0 - p)[:, None] * log_gamma[None, :]), 0.0)
    qdec = np.repeat(qdec, DK, axis=1)
    kdec = np.repeat(kdec, DK, axis=1)
    g_total = tuple(float(x) for x in np.exp(Lv * log_gamma))
    return (a_stack, lv, dec.astype(np.float32), qdec.astype(np.float32), kdec.astype(np.float32), g_total)


def _ln(x, g, b):
    mu = jnp.mean(x, -1, keepdims=True)
    xc = x - mu
    var = jnp.mean(xc * xc, -1, keepdims=True)
    return xc * lax.rsqrt(var + LN_EPS) * g + b


def _silu(x):
    return x * jax.nn.sigmoid(x)


def _expand(x, gs, mask):
    if gs == 1:
        return x
    return jnp.where(mask, jnp.concatenate([x] * gs, axis=1), 0.0)


def _col_bcast(rows, gs):
    outs = []
    for g in range(gs):
        outs.append(jnp.broadcast_to(rows[g:g + 1, :], (DK, DK)).T)
    return outs[0] if gs == 1 else jnp.concatenate(outs, axis=0)


def _boundary_rows(b, m):
    C, W = b.shape
    blk = 2 * m
    if blk >= 8:
        b3 = b.reshape(C // blk, blk, W)
        return jnp.broadcast_to(b3[:, m - 1:m, :], b3.shape).reshape(C, W)
    b3 = b.reshape(C // 8, 8, W)
    sub = lax.broadcasted_iota(jnp.int32, (1, 8, 1), 1)
    r = jnp.broadcast_to(b3[:, m - 1:m, :], b3.shape)
    for k in range(1, 8 // blk):
        r = jnp.where(sub >= k * blk, jnp.broadcast_to(b3[:, k * blk + m - 1:k * blk + m, :], b3.shape), r)
    return r.reshape(C, W)


def _mix_kernel(cfg, g_total, x_ref, eg_ref, eb_ref, win_ref, wo_ref, lb_ref, hgn_ref, rtg_ref, rtb_ref,
                l1g_ref, l1b_ref, cos_ref, sin_ref, a_ref, lv_ref, dec_ref, qdec_ref, kdec_ref,
                hg0_ref, rt0_ref, *rest):
    x1_ref, hg_ref, rt_ref = rest[-3:]
    C = cfg.C
    t = pl.program_id(1)

    @pl.when(t == 0)
    def _():
        hg_ref[...] = hg0_ref[...]
        rt_ref[...] = rt0_ref[...]

    def in_proj(s):
        x = x_ref[s * C:(s + 1) * C, :]
        if cfg.embed:
            x = _ln(x, eg_ref[...], eb_ref[...])
        return x, jnp.dot(x.astype(BF16), win_ref[...], preferred_element_type=F32)

    ahead = in_proj(0)
    for s in range(cfg.sub):
        x, proj = ahead
        if s + 1 < cfg.sub:
            ahead = in_proj(s + 1)
        rows = slice(s * C, (s + 1) * C)
        o_all = _mix_tile(cfg, g_total, proj, cos_ref[rows, :], sin_ref[rows, :], lb_ref, hgn_ref, rtg_ref,
                          rtb_ref, a_ref, lv_ref, dec_ref, qdec_ref, kdec_ref, hg_ref, rt_ref)
        mix = jnp.dot(o_all, wo_ref[...], preferred_element_type=F32)
        x1_ref[rows, :] = _ln(ALPHA * x + mix, l1g_ref[...], l1b_ref[...])


def _mix_tile(cfg, g_total, proj, cos, sin, lb_ref, hgn_ref, rtg_ref, rtb_ref, a_ref, lv_ref, dec_ref,
              qdec_ref, kdec_ref, hg_ref, rt_ref):
    C, L, Lv = cfg.C, cfg.L, cfg.Lv
    gs = C // L
    levels = _levels(Lv)
    heads = [slice(h * DK, (h + 1) * DK) for h in range(HEADS)]

    if gs > 1:
        row_seq = lax.broadcasted_iota(jnp.int32, (C, gs * DK), 0) // L
        lane_blk = lax.broadcasted_iota(jnp.int32, (C, gs * DK), 1) // DK
        seq_mask = row_seq == lane_blk
    else:
        seq_mask = None
    lv = lv_ref[...]


    rq, rk = proj[:, 4 * HW:5 * HW], proj[:, 5 * HW:6 * HW]
    rv, rgate = proj[:, 6 * HW:7 * HW], proj[:, 7 * HW:8 * HW]
    qdec, kdec = qdec_ref[...], kdec_ref[...]
    rt_q, rt_k, rt_sc = [], [], []
    for h, sl in enumerate(heads):
        qh = rq[:, sl] * cos[:, sl] + pltpu.roll(rq[:, sl], DK // 2, 1) * sin[:, sl]
        kh = (rk[:, sl] * cos[:, sl] + pltpu.roll(rk[:, sl], DK // 2, 1) * sin[:, sl]) * (DK ** -0.5)
        sc = lax.dot_general(qh.astype(BF16), kh.astype(BF16), NT_DIMS, preferred_element_type=F32)
        rt_q.append(qh)
        rt_k.append(kh)
        rt_sc.append((sc * dec_ref[h]).astype(BF16))

    hq, hf = proj[:, 0:HW], proj[:, HW:2 * HW]
    hv, hgate = proj[:, 2 * HW:3 * HW], proj[:, 3 * HW:4 * HW]
    lb = lb_ref[...]
    f_gate = lb + (1.0 - lb) * jax.nn.sigmoid(hf)
    lf = jnp.log(jnp.maximum(f_gate, F_EPS))
    kx = (1.0 - lb) * jax.nn.sigmoid(-hf)
    q = _silu(hq)
    if Lv < L:
        real = (lax.broadcasted_iota(jnp.int32, (C, 1), 0) % L) < Lv
        lf = jnp.where(real, lf, 0.0)
        kx = jnp.where(real, kx, 0.0)
    a_mat = a_ref[...]
    lf_hi = lf.astype(BF16)
    rem = lf - lf_hi.astype(F32)
    lf_mid = rem.astype(BF16)
    lf_lo = (rem - lf_mid.astype(F32)).astype(BF16)
    sums = (jnp.dot(a_mat, lf_hi, preferred_element_type=F32)
            + jnp.dot(a_mat, lf_mid, preferred_element_type=F32)
            + jnp.dot(a_mat, lf_lo, preferred_element_type=F32))
    b_incl = sums[0:C]
    q_in = q * jnp.exp(b_incl)
    k_out = kx * jnp.exp(sums[C:2 * C])
    if gs == 1:
        b_last = b_incl[Lv - 1:Lv, :]
    else:
        b_last = b_incl.reshape(gs, L, HW)[:, Lv - 1, :]
    d_rows = jnp.exp(b_last)

    qb, kb = q.astype(BF16), kx.astype(BF16)
    acc = []
    for sl in heads:
        sc = lax.dot_general(qb[:, sl], kb[:, sl], NT_DIMS, preferred_element_type=F32)
        acc.append(jnp.where(lv == 0, sc, 0.0))
    for j, m in enumerate(levels):
        e = jnp.exp(-jnp.abs(b_incl - _boundary_rows(b_incl, m)))
        qe, ke = (q * e).astype(BF16), (kx * e).astype(BF16)
        for h, sl in enumerate(heads):
            sc = lax.dot_general(qe[:, sl], ke[:, sl], NT_DIMS, preferred_element_type=F32)
            acc[h] = jnp.where(lv == j + 1, sc, acc[h])

    outs_rt = []
    for h, sl in enumerate(heads):
        vb = rv[:, sl].astype(BF16)
        s_old = rt_ref[:, h].reshape(gs * DK, DK)
        o = jnp.dot(rt_sc[h], vb, preferred_element_type=F32)
        o = o + jnp.dot(_expand(rt_q[h] * qdec[:, sl], gs, seq_mask).astype(BF16), s_old.astype(BF16),
                        preferred_element_type=F32)
        upd = lax.dot_general(_expand(rt_k[h] * kdec[:, sl], gs, seq_mask).astype(BF16), vb, TN_DIMS,
                              preferred_element_type=F32)
        rt_ref[:, h] = (g_total[h] * s_old + upd).reshape(gs, DK, DK)
        mu = jnp.mean(o, -1, keepdims=True)
        oc = o - mu
        o = oc * lax.rsqrt(jnp.mean(oc * oc, -1, keepdims=True) + LN_EPS)
        outs_rt.append((o * rtg_ref[:, sl] + rtb_ref[:, sl]) * _silu(rgate[:, sl]))

    outs_hg = []
    for h, sl in enumerate(heads):
        vb = hv[:, sl].astype(BF16)
        s_old = hg_ref[:, h].reshape(gs * DK, DK)
        o = jnp.dot(acc[h].astype(BF16), vb, preferred_element_type=F32)
        o = o + jnp.dot(_expand(q_in[:, sl], gs, seq_mask).astype(BF16), s_old.astype(BF16),
                        preferred_element_type=F32)
        upd = lax.dot_general(_expand(k_out[:, sl], gs, seq_mask).astype(BF16), vb, TN_DIMS,
                              preferred_element_type=F32)
        hg_ref[:, h] = (_col_bcast(d_rows[:, sl], gs) * s_old + upd).reshape(gs, DK, DK)
        o = o * lax.rsqrt(jnp.mean(o * o, -1, keepdims=True) + LN_EPS)
        outs_hg.append(o * hgn_ref[:, sl] * _silu(hgate[:, sl]))

    return jnp.concatenate(outs_hg + outs_rt, axis=1).astype(BF16)


def _ffn_kernel(cfg, x_ref, wa_ref, wb_ref, cw_ref, cb_ref, wd_ref, l2g_ref, l2b_ref, *rest):
    C, L, Lv = cfg.C, cfg.L, cfg.Lv
    gs = C // L
    if cfg.per_seq:
        fill1_ref, fill2_ref, y_ref, cs_ref, a_scr = rest
    else:
        init_ref, y_ref, cs_ref, a_scr = rest
    t = pl.program_id(1)

    x = x_ref[...]
    xb = x.astype(BF16)
    if cfg.per_seq:
        pos = lax.broadcasted_iota(jnp.int32, (C, 1), 0) % L
    else:
        @pl.when(t == 0)
        def _():
            a_scr[...] = jnp.zeros((CARRY, D_FF), F32)
            a_scr[CARRY - 2:CARRY, :] = init_ref[0]
        row8 = lax.broadcasted_iota(jnp.int32, (CARRY, 1), 0)

    def up_proj(j):
        cs = slice(j * FF_CHUNK, (j + 1) * FF_CHUNK)
        return (jnp.dot(xb, wa_ref[:, cs], preferred_element_type=F32),
                jnp.dot(xb, wb_ref[:, cs], preferred_element_type=F32))

    n_chunks = D_FF // FF_CHUNK
    ffn = jnp.zeros((C, D_MODEL), F32)
    ahead = up_proj(0)
    for j in range(n_chunks):
        cs = slice(j * FF_CHUNK, (j + 1) * FF_CHUNK)
        a, gate = ahead
        if j + 1 < n_chunks:
            ahead = up_proj(j + 1)
        a_m1 = pltpu.roll(a, 1, 0)
        a_m2 = pltpu.roll(a, 2, 0)
        if cfg.per_seq:
            a_m1 = jnp.where(pos >= 1, a_m1, fill1_ref[:, cs])
            a_m2 = jnp.where(pos >= 2, a_m2, fill2_ref[:, cs])
        else:
            prev = a_scr[:, cs]
            head1 = jnp.where(row8 == 0, prev[CARRY - 1:CARRY, :], a_m1[0:CARRY, :])
            head2 = jnp.where(row8 == 0, prev[CARRY - 2:CARRY - 1, :],
                              jnp.where(row8 == 1, prev[CARRY - 1:CARRY, :], a_m2[0:CARRY, :]))
            a_m1 = jnp.concatenate([head1, a_m1[CARRY:, :]], axis=0)
            a_m2 = jnp.concatenate([head2, a_m2[CARRY:, :]], axis=0)
            a_scr[:, cs] = a[C - CARRY:, :]
        conv = cb_ref[:, cs] + cw_ref[0:1, cs] * a_m2 + cw_ref[1:2, cs] * a_m1 + cw_ref[2:3, cs] * a
        hidden = (jax.nn.gelu(conv, approximate=True) * gate).astype(BF16)
        ffn = ffn + jnp.dot(hidden, wd_ref[cs, :], preferred_element_type=F32)
        if cfg.per_seq:
            a_seq = a.reshape(gs, L, FF_CHUNK)
            cs_ref[0, :, cs] = a_seq[:, Lv - 2, :]
            cs_ref[1, :, cs] = a_seq[:, Lv - 1, :]
        else:
            cs_ref[0, :, cs] = a[Lv - 2:Lv, :]
    y_ref[...] = _ln(ALPHA * x + ffn, l2g_ref[...], l2b_ref[...])


def _const_spec(shape):
    nd = len(shape)
    return pl.BlockSpec(shape, lambda b, t: (0,) * nd, pipeline_mode=pl.Buffered(1))


def _layer_spec(shape, layer):
    nd = len(shape)
    return pl.BlockSpec((None,) + tuple(shape), lambda b, t: (layer,) + (0,) * nd,
                        pipeline_mode=pl.Buffered(1))


def _params():
    return pltpu.CompilerParams(dimension_semantics=("arbitrary", "arbitrary"),
                                vmem_limit_bytes=VMEM_LIMIT_V7X)


def _mix_call(cfg, x, eg, eb, win, wo, lb, hgn, rtg, rtb, l1g, l1b, cos, sin, hg0, rt0, stacked=None):
    C, L, Lv, nb, nt = cfg.C, cfg.L, cfg.Lv, cfg.nb, cfg.nt
    gs = C // L
    a_stack, lv, dec, qdec, kdec, g_total = _tables(C, L, Lv)
    a_stack = jnp.asarray(a_stack, BF16)
    l = cfg.layer
    step_rows = cfg.sub * C
    rows = nb * nt * step_rows
    assert cfg.sub == 1 or gs == 1
    row_spec = pl.BlockSpec((step_rows, D_MODEL), lambda b, t: (b * nt + t, 0))
    if cfg.per_seq:
        st_spec = pl.BlockSpec((None, gs, HEADS, DK, DK), lambda b, t: (l, b, 0, 0, 0))
    else:
        st_spec = pl.BlockSpec((gs, HEADS, DK, DK), lambda b, t: (0, 0, 0, 0))
    in_specs = [
        row_spec,
        _const_spec((1, D_MODEL)), _const_spec((1, D_MODEL)),
        _layer_spec((D_MODEL, PROJ_COLS), l), _layer_spec((D_MODEL, D_MODEL), l),
        _layer_spec((1, HW), l), _layer_spec((1, HW), l), _layer_spec((1, HW), l), _layer_spec((1, HW), l),
        _layer_spec((1, D_MODEL), l), _layer_spec((1, D_MODEL), l),
        pl.BlockSpec((step_rows, HW), lambda b, t: (t, 0)), pl.BlockSpec((step_rows, HW), lambda b, t: (t, 0)),
        _const_spec(a_stack.shape), _const_spec(lv.shape), _const_spec(dec.shape),
        _const_spec(qdec.shape), _const_spec(kdec.shape),
        st_spec, st_spec,
    ]
    operands = [x, eg, eb, win, wo, lb, hgn, rtg, rtb, l1g, l1b, cos, sin,
                a_stack, jnp.asarray(lv), jnp.asarray(dec), jnp.asarray(qdec), jnp.asarray(kdec), hg0, rt0]
    aliases = {}
    if cfg.per_seq:
        out_st = pl.BlockSpec((None, gs, HEADS, DK, DK), lambda b, t: (l, b, 0, 0, 0))
        st_shape = jax.ShapeDtypeStruct((DEPTH, nb * gs, HEADS, DK, DK), F32)
        if stacked is not None:
            aliases = {len(operands): 1, len(operands) + 1: 2}
            in_specs += [pl.BlockSpec(memory_space=pl.ANY), pl.BlockSpec(memory_space=pl.ANY)]
            operands += list(stacked)
    else:
        out_st = pl.BlockSpec((gs, HEADS, DK, DK), lambda b, t: (b, 0, 0, 0))
        st_shape = jax.ShapeDtypeStruct((nb * gs, HEADS, DK, DK), F32)
    return pl.pallas_call(
        functools.partial(_mix_kernel, cfg, g_total),
        out_shape=(jax.ShapeDtypeStruct((rows, D_MODEL), F32), st_shape, st_shape),
        grid=(nb, nt),
        in_specs=in_specs,
        out_specs=(row_spec, out_st, out_st),
        input_output_aliases=aliases,
        compiler_params=_params(),
        name=f"mixer_l{l}_L{L}_nt{nt}",
    )(*operands)


def _ffn_call(cfg, x, wa, wb, cw, cb, wd, l2g, l2b, *conv_in):
    C, L, nb, nt = cfg.C, cfg.L, cfg.nb, cfg.nt
    gs = C // L
    l = cfg.layer
    rows = nb * nt * C
    row_spec = pl.BlockSpec((C, D_MODEL), lambda b, t: (b * nt + t, 0))
    in_specs = [
        row_spec,
        _layer_spec((D_MODEL, D_FF), l), _layer_spec((D_MODEL, D_FF), l),
        _layer_spec((CONV_W, D_FF), l), _layer_spec((1, D_FF), l),
        _layer_spec((D_FF, D_MODEL), l),
        _layer_spec((1, D_MODEL), l), _layer_spec((1, D_MODEL), l),
    ]
    if cfg.per_seq:
        fill_spec = pl.BlockSpec((C, D_FF), lambda b, t: (b, 0))
        in_specs += [fill_spec, fill_spec]
        cs_shape = jax.ShapeDtypeStruct((2, nb * gs, D_FF), F32)
        cs_spec = pl.BlockSpec((2, gs, D_FF), lambda b, t: (0, b, 0))
    else:
        in_specs += [pl.BlockSpec((1, CONV_W - 1, D_FF), lambda b, t: (0, 0, 0))]
        cs_shape = jax.ShapeDtypeStruct((nb, CONV_W - 1, D_FF), F32)
        cs_spec = pl.BlockSpec((1, CONV_W - 1, D_FF), lambda b, t: (b, 0, 0))
    return pl.pallas_call(
        functools.partial(_ffn_kernel, cfg),
        out_shape=(jax.ShapeDtypeStruct((rows, D_MODEL), F32), cs_shape),
        grid=(nb, nt),
        in_specs=in_specs,
        out_specs=(row_spec, cs_spec),
        scratch_shapes=[pltpu.VMEM((CARRY, D_FF), F32)],
        compiler_params=_params(),
        name=f"convffn_l{l}_L{L}_nt{nt}",
    )(x, wa, wb, cw, cb, wd, l2g, l2b, *conv_in)


def _rope_tables(pos):
    half = DK // 2
    inv = ROPE_BASE ** (-jnp.arange(half, dtype=F32) / half)
    ang = pos.astype(F32)[:, None] * inv[None, :]
    cos, sin = jnp.cos(ang), jnp.sin(ang)
    cos2 = jnp.concatenate([cos, cos], axis=1)
    sin2 = jnp.concatenate([-sin, sin], axis=1)
    return jnp.tile(cos2, (1, HEADS)), jnp.tile(sin2, (1, HEADS))


def kernel(x_prompt, x_sample, state_hgrn, state_ret, state_conv, meta_tokens, emb_ln_g, emb_ln_b, lb_logits, w_in, hg_norm_g, rt_gn_g, rt_gn_b, w_o, ln1_g, ln1_b, w_a, w_b, conv_w, conv_b, w_down, ln2_g, ln2_b):
    bsz, seq, _ = x_prompt.shape
    dec_b, dec_t, _ = x_sample.shape
    C = TILE
    assert seq % (C * MIX_SUB) == 0 and seq % FFN_TILE == 0 and N_META <= C and dec_t >= CONV_W - 1
    L_s = 8
    assert dec_t <= L_s and (dec_b * L_s) % C == 0

    sm = jax.nn.softmax(lb_logits.astype(F32), axis=0)
    lb_all = (jnp.cumsum(sm, axis=0) - sm[0:1])[:, None, :]
    win, wo = w_in.astype(BF16), w_o.astype(BF16)
    wa, wb, wd = w_a.astype(BF16), w_b.astype(BF16), w_down.astype(BF16)
    vec = lambda v: v[:, None, :]
    hgn, rtg, rtb = vec(hg_norm_g), vec(rt_gn_g), vec(rt_gn_b)
    l1g, l1b, l2g, l2b, cb = vec(ln1_g), vec(ln1_b), vec(ln2_g), vec(ln2_b), vec(conv_b)
    eg, eb = emb_ln_g[None, :], emb_ln_b[None, :]

    cos_m, sin_m = _rope_tables(jnp.arange(C, dtype=jnp.int32))
    cos_p, sin_p = _rope_tables(N_META + jnp.arange(seq, dtype=jnp.int32))
    cos_s, sin_s = _rope_tables(PAST_LEN + jnp.arange(C, dtype=jnp.int32) % L_s)

    xm = jnp.pad(meta_tokens.astype(F32), ((0, C - N_META), (0, 0)))
    xp = x_prompt.reshape(bsz * seq, D_MODEL)
    xs = jnp.pad(x_sample, ((0, 0), (0, L_s - dec_t), (0, 0))).reshape(dec_b * L_s, D_MODEL)
    zero_state = jnp.zeros((1, HEADS, DK, DK), F32)
    zero_conv = jnp.zeros((1, CONV_W - 1, D_FF), F32)

    hg_p, rt_p, cv_p, cv_s = [], [], [], []
    st_s = (jnp.zeros_like(state_hgrn, dtype=F32), jnp.zeros_like(state_ret, dtype=F32))
    for l in range(DEPTH):
        emb = l == 0
        cfg_m = Cfg(1, 1, C, C, N_META, emb, False, l)
        cfg_p = Cfg(bsz, seq // (C * MIX_SUB), C, C, C, emb, False, l, MIX_SUB)
        cfg_s = Cfg(dec_b * L_s // C, 1, C, L_s, dec_t, emb, True, l)
        mix_w = (eg, eb, win, wo, lb_all, hgn, rtg, rtb, l1g, l1b)
        ffn_w = (wa, wb, conv_w, cb, wd, l2g, l2b)

        xm1, hg_m, rt_m = _mix_call(cfg_m, xm, *mix_w, cos_m, sin_m, zero_state, zero_state)
        xm, cv_m = _ffn_call(cfg_m, xm1, *ffn_w, zero_conv)

        xp1, hg, rt = _mix_call(cfg_p, xp, *mix_w, cos_p, sin_p, hg_m, rt_m)
        cfg_pf = Cfg(bsz, seq // FFN_TILE, FFN_TILE, FFN_TILE, FFN_TILE, False, False, l)
        xp, cv = _ffn_call(cfg_pf, xp1, *ffn_w, cv_m)
        hg_p.append(hg)
        rt_p.append(rt)
        cv_p.append(cv)

        xs1, *st_s = _mix_call(cfg_s, xs, *mix_w, cos_s, sin_s, state_hgrn, state_ret, stacked=st_s)
        sc = state_conv[l]
        fill1 = jnp.pad(sc[:, 1:2], ((0, 0), (0, L_s - 1), (0, 0))).reshape(dec_b * L_s, D_FF)
        fill2 = jnp.pad(sc, ((0, 0), (0, L_s - 2), (0, 0))).reshape(dec_b * L_s, D_FF)
        xs, cv = _ffn_call(cfg_s, xs1, *ffn_w, fill1, fill2)
        cv_s.append(jnp.swapaxes(cv, 0, 1))

    y_prompt = xp.reshape(bsz, seq, D_MODEL)
    y_sample = xs.reshape(dec_b, L_s, D_MODEL)[:, :dec_t]
    hg_s, rt_s = st_s
    return (y_prompt, y_sample, jnp.stack(hg_p), hg_s, jnp.stack(rt_p), rt_s, jnp.stack(cv_p), jnp.stack(cv_s))
```

```python
import functools
from typing import NamedTuple

import numpy as np
import jax
import jax.numpy as jnp
from jax import lax
from jax.experimental import pallas as pl
from jax.experimental.pallas import tpu as pltpu

D_MODEL = 1024
DEPTH = 4
HEADS = 4
DK = 128
HW = HEADS * DK
PROJ_COLS = 8 * HW
N_META = 16
PAST_LEN = 16384
CONV_W = 3
D_FF = 2816
ROPE_BASE = 10000.0
LN_EPS = 1e-5
F_EPS = 1e-6
ALPHA = (2 * DEPTH) ** 0.25

TILE = 128
MIX_SUB = 4
N_PARTS = PROJ_COLS // HW
FFN_TILE = 512
FF_CHUNK = 256
DOWN_GROUP = 4
GELU_K0 = float((2.0 / np.pi) ** 0.5)
GELU_K1 = 0.044715 * GELU_K0
CARRY = 8
VMEM_LIMIT_V7X = 56 * 1024 * 1024

F32 = jnp.float32
BF16 = jnp.bfloat16
NT_DIMS = (((1,), (1,)), ((), ()))
TN_DIMS = (((0,), (0,)), ((), ()))


class Cfg(NamedTuple):
    nb: int
    nt: int
    C: int
    L: int
    Lv: int
    embed: bool
    per_seq: bool
    layer: int
    sub: int = 1


def _levels(Lv):
    return [m for m in (1, 2, 4, 8, 16, 32, 64, 128, 256) if m < Lv]


@functools.lru_cache(maxsize=None)
def _tables(C, L, Lv):
    r = np.arange(C)
    g, p = r // L, r % L
    same = g[:, None] == g[None, :]
    pt, pu = p[:, None], p[None, :]
    blocks = [same & (pu <= pt), same & (pu > pt)]
    lv = np.full((C, C), -1, np.int32)
    ok = same & (pt < Lv) & (pu < Lv)
    lv[ok & (pt == pu)] = 0
    for j, m in enumerate(_levels(Lv)):
        blk_t, blk_u = pt // (2 * m), pu // (2 * m)
        half_t, half_u = (pt // m) % 2, (pu // m) % 2
        lv[ok & same & (blk_t == blk_u) & (half_t == 1) & (half_u == 0)] = j + 1
    a_stack = np.concatenate(blocks, 0).astype(np.float32)
    a_stack = np.concatenate([a_stack, a_stack], 1)

    log_gamma = np.log1p(-np.exp2(-5.0 - np.arange(HEADS, dtype=np.float64)))
    rel = (pt - pu).astype(np.float64)
    causal = same & (pu <= pt) & (pu < Lv)
    dec = np.where(causal[None], np.exp(np.where(causal, rel, 0.0)[None] * log_gamma[:, None, None]), 0.0)
    qdec = np.exp((p + 1.0)[:, None] * log_gamma[None, :])
    kdec = np.where((p < Lv)[:, None], np.exp((Lv - 1.0 - p)[:, None] * log_gamma[None, :]), 0.0)
    qdec = np.repeat(qdec, DK, axis=1)
    kdec = np.repeat(kdec, DK, axis=1)
    g_total = tuple(float(x) for x in np.exp(Lv * log_gamma))
    return (a_stack, lv, dec.astype(np.float32), qdec.astype(np.float32), kdec.astype(np.float32), g_total)


def _ln(x, g, b):
    mu = jnp.mean(x, -1, keepdims=True)
    xc = x - mu
    var = jnp.mean(xc * xc, -1, keepdims=True)
    return xc * lax.rsqrt(var + LN_EPS) * g + b


def _silu(x):
    h = 0.5 * x
    return h + h * jnp.tanh(h)


def _expand(x, gs, mask):
    if gs == 1:
        return x
    return jnp.where(mask, jnp.concatenate([x] * gs, axis=1), 0.0)


def _col_bcast(rows, gs):
    outs = []
    for g in range(gs):
        outs.append(jnp.broadcast_to(rows[g:g + 1, :], (DK, DK)).T)
    return outs[0] if gs == 1 else jnp.concatenate(outs, axis=0)


def _boundary_rows(b, m):
    C, W = b.shape
    blk = 2 * m
    if blk >= 8:
        b3 = b.reshape(C // blk, blk, W)
        return jnp.broadcast_to(b3[:, m - 1:m, :], b3.shape).reshape(C, W)
    b3 = b.reshape(C // 8, 8, W)
    sub = lax.broadcasted_iota(jnp.int32, (1, 8, 1), 1)
    r = jnp.broadcast_to(b3[:, m - 1:m, :], b3.shape)
    for k in range(1, 8 // blk):
        r = jnp.where(sub >= k * blk, jnp.broadcast_to(b3[:, k * blk + m - 1:k * blk + m, :], b3.shape), r)
    return r.reshape(C, W)


def _mix_kernel(cfg, g_total, x_ref, eg_ref, eb_ref, win_ref, wo_ref, lb_ref, hgn_ref, rtg_ref, rtb_ref,
                l1g_ref, l1b_ref, cos_ref, sin_ref, a_ref, lv_ref, dec_ref, qdec_ref, kdec_ref,
                hg0_ref, rt0_ref, *rest):
    x1_ref, hg_ref, rt_ref = rest[-3:]
    C = cfg.C
    t = pl.program_id(1)

    @pl.when(t == 0)
    def _():
        hg_ref[...] = hg0_ref[...]
        rt_ref[...] = rt0_ref[...]

    def load_rows(s):
        x = x_ref[s * C:(s + 1) * C, :]
        if cfg.embed:
            x = _ln(x, eg_ref[...], eb_ref[...])
        return x, x.astype(BF16)

    def in_proj_part(xb, k):
        return jnp.dot(xb, win_ref[:, k * HW:(k + 1) * HW], preferred_element_type=F32)

    x, xb = load_rows(0)
    parts = [in_proj_part(xb, k) for k in range(N_PARTS)]
    for s in range(cfg.sub):
        rows = slice(s * C, (s + 1) * C)
        more = s + 1 < cfg.sub
        if more:
            x_next, xb_next = load_rows(s + 1)
        parts_next = []
        for o_all in _mix_tile(cfg, g_total, parts, cos_ref[rows, :], sin_ref[rows, :], lb_ref, hgn_ref, rtg_ref,
                               rtb_ref, a_ref, lv_ref, dec_ref, qdec_ref, kdec_ref, hg_ref, rt_ref):
            if more and len(parts_next) < N_PARTS:
                parts_next.append(in_proj_part(xb_next, len(parts_next)))
        mix = jnp.dot(o_all, wo_ref[...], preferred_element_type=F32)
        x1_ref[rows, :] = _ln(ALPHA * x + mix, l1g_ref[...], l1b_ref[...])
        if more:
            assert len(parts_next) == N_PARTS
            x, parts = x_next, parts_next


def _mix_tile(cfg, g_total, parts, cos, sin, lb_ref, hgn_ref, rtg_ref, rtb_ref, a_ref, lv_ref, dec_ref,
              qdec_ref, kdec_ref, hg_ref, rt_ref):
    C, L, Lv = cfg.C, cfg.L, cfg.Lv
    gs = C // L
    levels = _levels(Lv)
    hq, hf, hv, hgate, rq, rk, rv, rgate = parts
    heads = [slice(h * DK, (h + 1) * DK) for h in range(HEADS)]

    if gs > 1:
        row_seq = lax.broadcasted_iota(jnp.int32, (C, gs * DK), 0) // L
        lane_blk = lax.broadcasted_iota(jnp.int32, (C, gs * DK), 1) // DK
        seq_mask = row_seq == lane_blk
    else:
        seq_mask = None
    lv = lv_ref[...]


    qdec, kdec = qdec_ref[...], kdec_ref[...]
    rt_q, rt_k, rt_sc = [], [], []
    for h, sl in enumerate(heads):
        qh = rq[:, sl] * cos[:, sl] + pltpu.roll(rq[:, sl], DK // 2, 1) * sin[:, sl]
        kh = (rk[:, sl] * cos[:, sl] + pltpu.roll(rk[:, sl], DK // 2, 1) * sin[:, sl]) * (DK ** -0.5)
        sc = lax.dot_general(qh.astype(BF16), kh.astype(BF16), NT_DIMS, preferred_element_type=F32)
        rt_q.append(qh)
        rt_k.append(kh)
        rt_sc.append((sc * dec_ref[h]).astype(BF16))
    yield None

    lb = lb_ref[...]
    th = jnp.tanh(0.5 * hf)
    f_gate = lb + (1.0 - lb) * (0.5 + 0.5 * th)
    lf = jnp.log(jnp.maximum(f_gate, F_EPS))
    kx = (1.0 - lb) * (0.5 - 0.5 * th)
    q = _silu(hq)
    if Lv < L:
        real = (lax.broadcasted_iota(jnp.int32, (C, 1), 0) % L) < Lv
        lf = jnp.where(real, lf, 0.0)
        kx = jnp.where(real, kx, 0.0)
    lf_hi = lf.astype(BF16)
    rem = lf - lf_hi.astype(F32)
    lf_mid = rem.astype(BF16)
    lf_lo = (rem - lf_mid.astype(F32)).astype(BF16)
    sums = (jnp.dot(a_ref[...], jnp.concatenate([lf_hi, lf_mid], axis=0), preferred_element_type=F32)
            + jnp.dot(a_ref[:, 0:C], lf_lo, preferred_element_type=F32))
    b_incl = sums[0:C]
    q_in = q * jnp.exp(b_incl)
    k_out = kx * jnp.exp(sums[C:2 * C])
    if gs == 1:
        b_last = b_incl[Lv - 1:Lv, :]
    else:
        b_last = b_incl.reshape(gs, L, HW)[:, Lv - 1, :]
    d_rows = jnp.exp(b_last)

    qb, kb = q.astype(BF16), kx.astype(BF16)
    acc = []
    for sl in heads:
        sc = lax.dot_general(qb[:, sl], kb[:, sl], NT_DIMS, preferred_element_type=F32)
        acc.append(jnp.where(lv == 0, sc, 0.0))
    yield None
    for j, m in enumerate(levels):
        e = jnp.exp(-jnp.abs(b_incl - _boundary_rows(b_incl, m)))
        qe, ke = (q * e).astype(BF16), (kx * e).astype(BF16)
        for h, sl in enumerate(heads):
            sc = lax.dot_general(qe[:, sl], ke[:, sl], NT_DIMS, preferred_element_type=F32)
            acc[h] = jnp.where(lv == j + 1, sc, acc[h])
        yield None

    outs_rt = []
    for h, sl in enumerate(heads):
        vb = rv[:, sl].astype(BF16)
        s_old = rt_ref[:, h].reshape(gs * DK, DK)
        q_st = _expand(rt_q[h] * qdec[:, sl], gs, seq_mask).astype(BF16)
        o = jnp.dot(jnp.concatenate([rt_sc[h], q_st], axis=1),
                    jnp.concatenate([vb, s_old.astype(BF16)], axis=0), preferred_element_type=F32)
        upd = lax.dot_general(_expand(rt_k[h] * kdec[:, sl], gs, seq_mask).astype(BF16), vb, TN_DIMS,
                              preferred_element_type=F32)
        rt_ref[:, h] = (g_total[h] * s_old + upd).reshape(gs, DK, DK)
        mu = jnp.mean(o, -1, keepdims=True)
        oc = o - mu
        o = oc * lax.rsqrt(jnp.mean(oc * oc, -1, keepdims=True) + LN_EPS)
        outs_rt.append((o * rtg_ref[:, sl] + rtb_ref[:, sl]) * _silu(rgate[:, sl]))
        yield None

    outs_hg = []
    for h, sl in enumerate(heads):
        vb = hv[:, sl].astype(BF16)
        s_old = hg_ref[:, h].reshape(gs * DK, DK)
        q_st = _expand(q_in[:, sl], gs, seq_mask).astype(BF16)
        o = jnp.dot(jnp.concatenate([acc[h].astype(BF16), q_st], axis=1),
                    jnp.concatenate([vb, s_old.astype(BF16)], axis=0), preferred_element_type=F32)
        upd = lax.dot_general(_expand(k_out[:, sl], gs, seq_mask).astype(BF16), vb, TN_DIMS,
                              preferred_element_type=F32)
        hg_ref[:, h] = (_col_bcast(d_rows[:, sl], gs) * s_old + upd).reshape(gs, DK, DK)
        o = o * lax.rsqrt(jnp.mean(o * o, -1, keepdims=True) + LN_EPS)
        outs_hg.append(o * hgn_ref[:, sl] * _silu(hgate[:, sl]))
        if h + 1 < HEADS:
            yield None

    yield jnp.concatenate(outs_hg + outs_rt, axis=1).astype(BF16)


def _ffn_kernel(cfg, x_ref, wa_ref, wb_ref, cw_ref, cb_ref, wd_ref, l2g_ref, l2b_ref, *rest):
    C, L, Lv = cfg.C, cfg.L, cfg.Lv
    gs = C // L
    if cfg.per_seq:
        fill1_ref, fill2_ref, y_ref, cs_ref, a_scr = rest
    else:
        init_ref, y_ref, cs_ref, a_scr = rest
    t = pl.program_id(1)

    x = x_ref[...]
    xb = x.astype(BF16)
    if cfg.per_seq:
        pos = lax.broadcasted_iota(jnp.int32, (C, 1), 0) % L
    else:
        @pl.when(t == 0)
        def _():
            a_scr[...] = jnp.zeros((CARRY, D_FF), F32)
            a_scr[CARRY - 2:CARRY, :] = init_ref[0]
        row8 = lax.broadcasted_iota(jnp.int32, (CARRY, 1), 0)

    def up_proj(j):
        cs = slice(j * FF_CHUNK, (j + 1) * FF_CHUNK)
        return (jnp.dot(xb, wa_ref[:, cs], preferred_element_type=F32),
                jnp.dot(xb, wb_ref[:, cs], preferred_element_type=F32))

    n_chunks = D_FF // FF_CHUNK
    ffn = None
    pending = []
    ahead = up_proj(0)
    for j in range(n_chunks):
        cs = slice(j * FF_CHUNK, (j + 1) * FF_CHUNK)
        a, gate = ahead
        if j + 1 < n_chunks:
            ahead = up_proj(j + 1)
        a_m1 = pltpu.roll(a, 1, 0)
        a_m2 = pltpu.roll(a, 2, 0)
        if cfg.per_seq:
            a_m1 = jnp.where(pos >= 1, a_m1, fill1_ref[:, cs])
            a_m2 = jnp.where(pos >= 2, a_m2, fill2_ref[:, cs])
        else:
            prev = a_scr[:, cs]
            head1 = jnp.where(row8 == 0, prev[CARRY - 1:CARRY, :], a_m1[0:CARRY, :])
            head2 = jnp.where(row8 == 0, prev[CARRY - 2:CARRY - 1, :],
                              jnp.where(row8 == 1, prev[CARRY - 1:CARRY, :], a_m2[0:CARRY, :]))
            a_m1 = jnp.concatenate([head1, a_m1[CARRY:, :]], axis=0)
            a_m2 = jnp.concatenate([head2, a_m2[CARRY:, :]], axis=0)
            a_scr[:, cs] = a[C - CARRY:, :]
        conv = cb_ref[:, cs] + cw_ref[0:1, cs] * a_m2 + cw_ref[1:2, cs] * a_m1 + cw_ref[2:3, cs] * a
        th = jnp.tanh(conv * (GELU_K0 + GELU_K1 * (conv * conv)))
        pending.append(((conv + conv * th) * gate).astype(BF16))
        if len(pending) == DOWN_GROUP or j == n_chunks - 1:
            lo = (j + 1 - len(pending)) * FF_CHUNK
            hidden = pending[0] if len(pending) == 1 else jnp.concatenate(pending, axis=1)
            part = jnp.dot(hidden, wd_ref[lo:(j + 1) * FF_CHUNK, :], preferred_element_type=F32)
            ffn = part if ffn is None else ffn + part
            pending = []
        if cfg.per_seq:
            a_seq = a.reshape(gs, L, FF_CHUNK)
            cs_ref[0, :, cs] = a_seq[:, Lv - 2, :]
            cs_ref[1, :, cs] = a_seq[:, Lv - 1, :]
        else:
            cs_ref[0, :, cs] = a[Lv - 2:Lv, :]
    y_ref[...] = _ln(ALPHA * x + ffn, l2g_ref[...], l2b_ref[...])


def _const_spec(shape):
    nd = len(shape)
    return pl.BlockSpec(shape, lambda b, t: (0,) * nd, pipeline_mode=pl.Buffered(1))


def _layer_spec(shape, layer):
    nd = len(shape)
    return pl.BlockSpec((None,) + tuple(shape), lambda b, t: (layer,) + (0,) * nd,
                        pipeline_mode=pl.Buffered(1))


def _params():
    return pltpu.CompilerParams(dimension_semantics=("arbitrary", "arbitrary"),
                                vmem_limit_bytes=VMEM_LIMIT_V7X)


def _mix_call(cfg, x, eg, eb, win, wo, lb, hgn, rtg, rtb, l1g, l1b, cos, sin, hg0, rt0, stacked=None):
    C, L, Lv, nb, nt = cfg.C, cfg.L, cfg.Lv, cfg.nb, cfg.nt
    gs = C // L
    a_stack, lv, dec, qdec, kdec, g_total = _tables(C, L, Lv)
    a_stack = jnp.asarray(a_stack, BF16)
    l = cfg.layer
    step_rows = cfg.sub * C
    rows = nb * nt * step_rows
    assert cfg.sub == 1 or gs == 1
    row_spec = pl.BlockSpec((step_rows, D_MODEL), lambda b, t: (b * nt + t, 0))
    if cfg.per_seq:
        st_spec = pl.BlockSpec((None, gs, HEADS, DK, DK), lambda b, t: (l, b, 0, 0, 0))
    else:
        st_spec = pl.BlockSpec((gs, HEADS, DK, DK), lambda b, t: (0, 0, 0, 0))
    in_specs = [
        row_spec,
        _const_spec((1, D_MODEL)), _const_spec((1, D_MODEL)),
        _layer_spec((D_MODEL, PROJ_COLS), l), _layer_spec((D_MODEL, D_MODEL), l),
        _layer_spec((1, HW), l), _layer_spec((1, HW), l), _layer_spec((1, HW), l), _layer_spec((1, HW), l),
        _layer_spec((1, D_MODEL), l), _layer_spec((1, D_MODEL), l),
        pl.BlockSpec((step_rows, HW), lambda b, t: (t, 0)), pl.BlockSpec((step_rows, HW), lambda b, t: (t, 0)),
        _const_spec(a_stack.shape), _const_spec(lv.shape), _const_spec(dec.shape),
        _const_spec(qdec.shape), _const_spec(kdec.shape),
        st_spec, st_spec,
    ]
    operands = [x, eg, eb, win, wo, lb, hgn, rtg, rtb, l1g, l1b, cos, sin,
                a_stack, jnp.asarray(lv), jnp.asarray(dec), jnp.asarray(qdec), jnp.asarray(kdec), hg0, rt0]
    aliases = {}
    if cfg.per_seq:
        out_st = pl.BlockSpec((None, gs, HEADS, DK, DK), lambda b, t: (l, b, 0, 0, 0))
        st_shape = jax.ShapeDtypeStruct((DEPTH, nb * gs, HEADS, DK, DK), F32)
        if stacked is not None:
            aliases = {len(operands): 1, len(operands) + 1: 2}
            in_specs += [pl.BlockSpec(memory_space=pl.ANY), pl.BlockSpec(memory_space=pl.ANY)]
            operands += list(stacked)
    else:
        out_st = pl.BlockSpec((gs, HEADS, DK, DK), lambda b, t: (b, 0, 0, 0))
        st_shape = jax.ShapeDtypeStruct((nb * gs, HEADS, DK, DK), F32)
    return pl.pallas_call(
        functools.partial(_mix_kernel, cfg, g_total),
        out_shape=(jax.ShapeDtypeStruct((rows, D_MODEL), F32), st_shape, st_shape),
        grid=(nb, nt),
        in_specs=in_specs,
        out_specs=(row_spec, out_st, out_st),
        input_output_aliases=aliases,
        compiler_params=_params(),
        name=f"mixer_l{l}_L{L}_nt{nt}",
    )(*operands)


def _ffn_call(cfg, x, wa, wb, cw, cb, wd, l2g, l2b, *conv_in):
    C, L, nb, nt = cfg.C, cfg.L, cfg.nb, cfg.nt
    gs = C // L
    l = cfg.layer
    rows = nb * nt * C
    row_spec = pl.BlockSpec((C, D_MODEL), lambda b, t: (b * nt + t, 0))
    in_specs = [
        row_spec,
        _layer_spec((D_MODEL, D_FF), l), _layer_spec((D_MODEL, D_FF), l),
        _layer_spec((CONV_W, D_FF), l), _layer_spec((1, D_FF), l),
        _layer_spec((D_FF, D_MODEL), l),
        _layer_spec((1, D_MODEL), l), _layer_spec((1, D_MODEL), l),
    ]
    if cfg.per_seq:
        fill_spec = pl.BlockSpec((C, D_FF), lambda b, t: (b, 0))
        in_specs += [fill_spec, fill_spec]
        cs_shape = jax.ShapeDtypeStruct((2, nb * gs, D_FF), F32)
        cs_spec = pl.BlockSpec((2, gs, D_FF), lambda b, t: (0, b, 0))
    else:
        in_specs += [pl.BlockSpec((1, CONV_W - 1, D_FF), lambda b, t: (0, 0, 0))]
        cs_shape = jax.ShapeDtypeStruct((nb, CONV_W - 1, D_FF), F32)
        cs_spec = pl.BlockSpec((1, CONV_W - 1, D_FF), lambda b, t: (b, 0, 0))
    return pl.pallas_call(
        functools.partial(_ffn_kernel, cfg),
        out_shape=(jax.ShapeDtypeStruct((rows, D_MODEL), F32), cs_shape),
        grid=(nb, nt),
        in_specs=in_specs,
        out_specs=(row_spec, cs_spec),
        scratch_shapes=[pltpu.VMEM((CARRY, D_FF), F32)],
        compiler_params=_params(),
        name=f"convffn_l{l}_L{L}_nt{nt}",
    )(x, wa, wb, cw, cb, wd, l2g, l2b, *conv_in)


def _rope_tables(pos):
    half = DK // 2
    inv = ROPE_BASE ** (-jnp.arange(half, dtype=F32) / half)
    ang = pos.astype(F32)[:, None] * inv[None, :]
    cos, sin = jnp.cos(ang), jnp.sin(ang)
    cos2 = jnp.concatenate([cos, cos], axis=1)
    sin2 = jnp.concatenate([-sin, sin], axis=1)
    return jnp.tile(cos2, (1, HEADS)), jnp.tile(sin2, (1, HEADS))


def kernel(x_prompt, x_sample, state_hgrn, state_ret, state_conv, meta_tokens, emb_ln_g, emb_ln_b, lb_logits, w_in, hg_norm_g, rt_gn_g, rt_gn_b, w_o, ln1_g, ln1_b, w_a, w_b, conv_w, conv_b, w_down, ln2_g, ln2_b):
    bsz, seq, _ = x_prompt.shape
    dec_b, dec_t, _ = x_sample.shape
    C = TILE
    assert seq % (C * MIX_SUB) == 0 and seq % FFN_TILE == 0 and N_META <= C and dec_t >= CONV_W - 1
    L_s = 8
    assert dec_t <= L_s and (dec_b * L_s) % C == 0

    sm = jax.nn.softmax(lb_logits.astype(F32), axis=0)
    lb_all = (jnp.cumsum(sm, axis=0) - sm[0:1])[:, None, :]
    win, wo = w_in.astype(BF16), w_o.astype(BF16)
    wa, wb, wd = w_a.astype(BF16), (0.5 * w_b).astype(BF16), w_down.astype(BF16)
    vec = lambda v: v[:, None, :]
    hgn, rtg, rtb = vec(hg_norm_g), vec(rt_gn_g), vec(rt_gn_b)
    l1g, l1b, l2g, l2b, cb = vec(ln1_g), vec(ln1_b), vec(ln2_g), vec(ln2_b), vec(conv_b)
    eg, eb = emb_ln_g[None, :], emb_ln_b[None, :]

    cos_m, sin_m = _rope_tables(jnp.arange(C, dtype=jnp.int32))
    cos_p, sin_p = _rope_tables(N_META + jnp.arange(seq, dtype=jnp.int32))
    cos_s, sin_s = _rope_tables(PAST_LEN + jnp.arange(C, dtype=jnp.int32) % L_s)

    xm = jnp.pad(meta_tokens.astype(F32), ((0, C - N_META), (0, 0)))
    xp = x_prompt.reshape(bsz * seq, D_MODEL)
    xs = jnp.pad(x_sample, ((0, 0), (0, L_s - dec_t), (0, 0))).reshape(dec_b * L_s, D_MODEL)
    zero_state = jnp.zeros((1, HEADS, DK, DK), F32)
    zero_conv = jnp.zeros((1, CONV_W - 1, D_FF), F32)

    hg_p, rt_p, cv_p, cv_s = [], [], [], []
    st_s = (jnp.zeros_like(state_hgrn, dtype=F32), jnp.zeros_like(state_ret, dtype=F32))
    for l in range(DEPTH):
        emb = l == 0
        cfg_m = Cfg(1, 1, C, C, N_META, emb, False, l)
        cfg_p = Cfg(bsz, seq // (C * MIX_SUB), C, C, C, emb, False, l, MIX_SUB)
        cfg_s = Cfg(dec_b * L_s // C, 1, C, L_s, dec_t, emb, True, l)
        mix_w = (eg, eb, win, wo, lb_all, hgn, rtg, rtb, l1g, l1b)
        ffn_w = (wa, wb, conv_w, cb, wd, l2g, l2b)

        xm1, hg_m, rt_m = _mix_call(cfg_m, xm, *mix_w, cos_m, sin_m, zero_state, zero_state)
        xm, cv_m = _ffn_call(cfg_m, xm1, *ffn_w, zero_conv)

        xp1, hg, rt = _mix_call(cfg_p, xp, *mix_w, cos_p, sin_p, hg_m, rt_m)
        cfg_pf = Cfg(bsz, seq // FFN_TILE, FFN_TILE, FFN_TILE, FFN_TILE, False, False, l)
        xp, cv = _ffn_call(cfg_pf, xp1, *ffn_w, cv_m)
        hg_p.append(hg)
        rt_p.append(rt)
        cv_p.append(cv)

        xs1, *st_s = _mix_call(cfg_s, xs, *mix_w, cos_s, sin_s, state_hgrn, state_ret, stacked=st_s)
        sc = state_conv[l]
        fill1 = jnp.pad(sc[:, 1:2], ((0, 0), (0, L_s - 1), (0, 0))).reshape(dec_b * L_s, D_FF)
        fill2 = jnp.pad(sc, ((0, 0), (0, L_s - 2), (0, 0))).reshape(dec_b * L_s, D_FF)
        xs, cv = _ffn_call(cfg_s, xs1, *ffn_w, fill1, fill2)
        cv_s.append(jnp.swapaxes(cv, 0, 1))

    y_prompt = xp.reshape(bsz, seq, D_MODEL)
    y_sample = xs.reshape(dec_b, L_s, D_MODEL)[:, :dec_t]
    hg_s, rt_s = st_s
    return (y_prompt, y_sample, jnp.stack(hg_p), hg_s, jnp.stack(rt_p), rt_s, jnp.stack(cv_p), jnp.stack(cv_s))
```

```python
import functools
from typing import NamedTuple

import numpy as np
import jax
import jax.numpy as jnp
from jax import lax
from jax.experimental import pallas as pl
from jax.experimental.pallas import tpu as pltpu

D_MODEL = 1024
DEPTH = 4
HEADS = 4
DK = 128
HW = HEADS * DK
PROJ_COLS = 8 * HW
N_META = 16
PAST_LEN = 16384
CONV_W = 3
D_FF = 2816
ROPE_BASE = 10000.0
LN_EPS = 1e-5
F_EPS = 1e-6
ALPHA = (2 * DEPTH) ** 0.25

TILE = 128
MIX_SUB = 4
N_PARTS = PROJ_COLS // HW
FFN_TILE = 1024
FF_CHUNK = 256
DOWN_GROUP = 4
GELU_K0 = float((2.0 / np.pi) ** 0.5)
GELU_K1 = 0.044715 * GELU_K0
CARRY = 8
VMEM_LIMIT_V7X = 56 * 1024 * 1024

F32 = jnp.float32
BF16 = jnp.bfloat16
NT_DIMS = (((1,), (1,)), ((), ()))
TN_DIMS = (((0,), (0,)), ((), ()))


class Cfg(NamedTuple):
    nb: int
    nt: int
    C: int
    L: int
    Lv: int
    embed: bool
    per_seq: bool
    layer: int
    sub: int = 1


def _levels(Lv):
    return [m for m in (1, 2, 4, 8, 16, 32, 64, 128, 256) if m < Lv]


@functools.lru_cache(maxsize=None)
def _tables(C, L, Lv):
    r = np.arange(C)
    g, p = r // L, r % L
    same = g[:, None] == g[None, :]
    pt, pu = p[:, None], p[None, :]
    blocks = [same & (pu <= pt), same & (pu > pt)]
    lv = np.full((C, C), -1, np.int32)
    ok = same & (pt < Lv) & (pu < Lv)
    lv[ok & (pt == pu)] = 0
    for j, m in enumerate(_levels(Lv)):
        blk_t, blk_u = pt // (2 * m), pu // (2 * m)
        half_t, half_u = (pt // m) % 2, (pu // m) % 2
        lv[ok & same & (blk_t == blk_u) & (half_t == 1) & (half_u == 0)] = j + 1
    a_stack = np.concatenate(blocks, 0).astype(np.float32)
    a_stack = np.concatenate([a_stack, a_stack], 1)

    log_gamma = np.log1p(-np.exp2(-5.0 - np.arange(HEADS, dtype=np.float64)))
    rel = (pt - pu).astype(np.float64)
    causal = same & (pu <= pt) & (pu < Lv)
    dec = np.where(causal[None], np.exp(np.where(causal, rel, 0.0)[None] * log_gamma[:, None, None]), 0.0)
    qdec = np.exp((p + 1.0)[:, None] * log_gamma[None, :])
    kdec = np.where((p < Lv)[:, None], np.exp((Lv - 1.0 - p)[:, None] * log_gamma[None, :]), 0.0)
    qdec = np.repeat(qdec, DK, axis=1)
    kdec = np.repeat(kdec, DK, axis=1)
    g_total = tuple(float(x) for x in np.exp(Lv * log_gamma))
    return (a_stack, lv, dec.astype(np.float32), qdec.astype(np.float32), kdec.astype(np.float32), g_total)


def _ln(x, g, b):
    mu = jnp.mean(x, -1, keepdims=True)
    xc = x - mu
    var = jnp.mean(xc * xc, -1, keepdims=True)
    return xc * lax.rsqrt(var + LN_EPS) * g + b


def _silu(x):
    h = 0.5 * x
    return h + h * jnp.tanh(h)


def _expand(x, gs, mask):
    if gs == 1:
        return x
    return jnp.where(mask, jnp.concatenate([x] * gs, axis=1), jnp.zeros((), x.dtype))


def _col_bcast(rows, gs):
    outs = []
    for g in range(gs):
        outs.append(jnp.broadcast_to(rows[g:g + 1, :], (DK, DK)).T)
    return outs[0] if gs == 1 else jnp.concatenate(outs, axis=0)


def _boundary_rows(b, m):
    C, W = b.shape
    blk = 2 * m
    if blk >= 8:
        b3 = b.reshape(C // blk, blk, W)
        return jnp.broadcast_to(b3[:, m - 1:m, :], b3.shape).reshape(C, W)
    b3 = b.reshape(C // 8, 8, W)
    sub = lax.broadcasted_iota(jnp.int32, (1, 8, 1), 1)
    r = jnp.broadcast_to(b3[:, m - 1:m, :], b3.shape)
    for k in range(1, 8 // blk):
        r = jnp.where(sub >= k * blk, jnp.broadcast_to(b3[:, k * blk + m - 1:k * blk + m, :], b3.shape), r)
    return r.reshape(C, W)


def _mix_kernel(cfg, g_total, x_ref, eg_ref, eb_ref, win_ref, wo_ref, lb_ref, hgn_ref, rtg_ref, rtb_ref,
                l1g_ref, l1b_ref, cos_ref, sin_ref, a_ref, lv_ref, dec_ref, qdec_ref, kdec_ref,
                hg0_ref, rt0_ref, *rest):
    x1_ref, hg_ref, rt_ref = rest[-3:]
    C = cfg.C
    t = pl.program_id(1)

    @pl.when(t == 0)
    def _():
        hg_ref[...] = hg0_ref[...]
        rt_ref[...] = rt0_ref[...]

    def load_rows(s):
        x = x_ref[s * C:(s + 1) * C, :]
        if cfg.embed:
            x = _ln(x, eg_ref[...], eb_ref[...])
        return x, x.astype(BF16)

    def in_proj_part(xb, k):
        return jnp.dot(xb, win_ref[:, k * HW:(k + 1) * HW], preferred_element_type=F32)

    x, xb = load_rows(0)
    parts = [in_proj_part(xb, k) for k in range(N_PARTS)]
    for s in range(cfg.sub):
        rows = slice(s * C, (s + 1) * C)
        more = s + 1 < cfg.sub
        if more:
            x_next, xb_next = load_rows(s + 1)
        parts_next = []
        for o_all in _mix_tile(cfg, g_total, parts, cos_ref[rows, :], sin_ref[rows, :], lb_ref, hgn_ref, rtg_ref,
                               rtb_ref, a_ref, lv_ref, dec_ref, qdec_ref, kdec_ref, hg_ref, rt_ref):
            if more and len(parts_next) < N_PARTS:
                parts_next.append(in_proj_part(xb_next, len(parts_next)))
        mix = jnp.dot(o_all, wo_ref[...], preferred_element_type=F32)
        x1_ref[rows, :] = _ln(ALPHA * x + mix, l1g_ref[...], l1b_ref[...])
        if more:
            assert len(parts_next) == N_PARTS
            x, parts = x_next, parts_next


def _mix_tile(cfg, g_total, parts, cos, sin, lb_ref, hgn_ref, rtg_ref, rtb_ref, a_ref, lv_ref, dec_ref,
              qdec_ref, kdec_ref, hg_ref, rt_ref):
    C, L, Lv = cfg.C, cfg.L, cfg.Lv
    gs = C // L
    levels = _levels(Lv)
    hq, hf, hv, hgate, rq, rk, rv, rgate = parts
    heads = [slice(h * DK, (h + 1) * DK) for h in range(HEADS)]

    if gs > 1:
        row_seq = lax.broadcasted_iota(jnp.int32, (C, gs * DK), 0) // L
        lane_blk = lax.broadcasted_iota(jnp.int32, (C, gs * DK), 1) // DK
        seq_mask = row_seq == lane_blk
    else:
        seq_mask = None
    lv = lv_ref[...]


    qdec, kdec = qdec_ref[...], kdec_ref[...]
    rt_q, rt_k, rt_sc = [], [], []
    for h, sl in enumerate(heads):
        qh = rq[:, sl] * cos[:, sl] + pltpu.roll(rq[:, sl], DK // 2, 1) * sin[:, sl]
        kh = (rk[:, sl] * cos[:, sl] + pltpu.roll(rk[:, sl], DK // 2, 1) * sin[:, sl]) * (DK ** -0.5)
        sc = lax.dot_general(qh.astype(BF16), kh.astype(BF16), NT_DIMS, preferred_element_type=F32)
        rt_q.append(qh)
        rt_k.append(kh)
        rt_sc.append((sc * dec_ref[h]).astype(BF16))
    yield None

    lb = lb_ref[...]
    th = jnp.tanh(0.5 * hf)
    f_gate = lb + (1.0 - lb) * (0.5 + 0.5 * th)
    lf = jnp.log(jnp.maximum(f_gate, F_EPS))
    kx = (1.0 - lb) * (0.5 - 0.5 * th)
    q = _silu(hq)
    if Lv < L:
        real = (lax.broadcasted_iota(jnp.int32, (C, 1), 0) % L) < Lv
        lf = jnp.where(real, lf, 0.0)
        kx = jnp.where(real, kx, 0.0)
    lf_hi = lf.astype(BF16)
    rem = lf - lf_hi.astype(F32)
    lf_mid = rem.astype(BF16)
    lf_lo = (rem - lf_mid.astype(F32)).astype(BF16)
    sums = (jnp.dot(a_ref[...], jnp.concatenate([lf_hi, lf_mid], axis=0), preferred_element_type=F32)
            + jnp.dot(a_ref[:, 0:C], lf_lo, preferred_element_type=F32))
    b_incl = sums[0:C]
    q_in = q * jnp.exp(b_incl)
    k_out = kx * jnp.exp(sums[C:2 * C])
    if gs == 1:
        b_last = b_incl[Lv - 1:Lv, :]
    else:
        b_last = b_incl.reshape(gs, L, HW)[:, Lv - 1, :]
    d_rows = jnp.exp(b_last)

    qb, kb = q.astype(BF16), kx.astype(BF16)
    level_sc = [[lax.dot_general(qb[:, sl], kb[:, sl], NT_DIMS, preferred_element_type=F32)] for sl in heads]
    yield None
    for m in levels:
        e = jnp.exp(-jnp.abs(b_incl - _boundary_rows(b_incl, m))).astype(BF16)
        qe, ke = qb * e, kb * e
        for h, sl in enumerate(heads):
            level_sc[h].append(lax.dot_general(qe[:, sl], ke[:, sl], NT_DIMS, preferred_element_type=F32))
        yield None

    outs_rt = []
    for h, sl in enumerate(heads):
        vb = rv[:, sl].astype(BF16)
        s_old = rt_ref[:, h].reshape(gs * DK, DK)
        q_st = _expand((rt_q[h] * qdec[:, sl]).astype(BF16), gs, seq_mask)
        o = jnp.dot(jnp.concatenate([rt_sc[h], q_st], axis=1),
                    jnp.concatenate([vb, s_old.astype(BF16)], axis=0), preferred_element_type=F32)
        upd = lax.dot_general(_expand((rt_k[h] * kdec[:, sl]).astype(BF16), gs, seq_mask), vb, TN_DIMS,
                              preferred_element_type=F32)
        rt_ref[:, h] = (g_total[h] * s_old + upd).reshape(gs, DK, DK)
        mu = jnp.mean(o, -1, keepdims=True)
        oc = o - mu
        o = oc * lax.rsqrt(jnp.mean(oc * oc, -1, keepdims=True) + LN_EPS)
        outs_rt.append((o * rtg_ref[:, sl] + rtb_ref[:, sl]) * _silu(rgate[:, sl]))
        yield None

    outs_hg = []
    for h, sl in enumerate(heads):
        vb = hv[:, sl].astype(BF16)
        s_old = hg_ref[:, h].reshape(gs * DK, DK)
        q_st = _expand(q_in[:, sl].astype(BF16), gs, seq_mask)
        scores = jnp.where(lv == 0, level_sc[h][0], 0.0)
        for j in range(len(levels)):
            scores = jnp.where(lv == j + 1, level_sc[h][j + 1], scores)
        o = jnp.dot(jnp.concatenate([scores.astype(BF16), q_st], axis=1),
                    jnp.concatenate([vb, s_old.astype(BF16)], axis=0), preferred_element_type=F32)
        upd = lax.dot_general(_expand(k_out[:, sl].astype(BF16), gs, seq_mask), vb, TN_DIMS,
                              preferred_element_type=F32)
        hg_ref[:, h] = (_col_bcast(d_rows[:, sl], gs) * s_old + upd).reshape(gs, DK, DK)
        o = o * lax.rsqrt(jnp.mean(o * o, -1, keepdims=True) + LN_EPS)
        outs_hg.append(o * hgn_ref[:, sl] * _silu(hgate[:, sl]))
        if h + 1 < HEADS:
            yield None

    yield jnp.concatenate(outs_hg + outs_rt, axis=1).astype(BF16)


def _ffn_kernel(cfg, x_ref, wa_ref, wb_ref, cw_ref, cb_ref, wd_ref, l2g_ref, l2b_ref, *rest):
    C, L, Lv = cfg.C, cfg.L, cfg.Lv
    gs = C // L
    if cfg.per_seq:
        fill1_ref, fill2_ref, y_ref, cs_ref, a_scr = rest
    else:
        init_ref, y_ref, cs_ref, a_scr = rest
    t = pl.program_id(1)

    x = x_ref[...]
    xb = x.astype(BF16)
    if cfg.per_seq:
        pos = lax.broadcasted_iota(jnp.int32, (C, 1), 0) % L
    else:
        @pl.when(t == 0)
        def _():
            a_scr[...] = jnp.zeros((CARRY, D_FF), F32)
            a_scr[CARRY - 2:CARRY, :] = init_ref[0]
        row8 = lax.broadcasted_iota(jnp.int32, (CARRY, 1), 0)

    def up_proj(j):
        cs = slice(j * FF_CHUNK, (j + 1) * FF_CHUNK)
        return (jnp.dot(xb, wa_ref[:, cs], preferred_element_type=F32),
                jnp.dot(xb, wb_ref[:, cs], preferred_element_type=F32))

    n_chunks = D_FF // FF_CHUNK
    ffn = None
    pending = []
    ahead = up_proj(0)
    for j in range(n_chunks):
        cs = slice(j * FF_CHUNK, (j + 1) * FF_CHUNK)
        a, gate = ahead
        if j + 1 < n_chunks:
            ahead = up_proj(j + 1)
        a_m1 = pltpu.roll(a, 1, 0)
        a_m2 = pltpu.roll(a, 2, 0)
        if cfg.per_seq:
            a_m1 = jnp.where(pos >= 1, a_m1, fill1_ref[:, cs])
            a_m2 = jnp.where(pos >= 2, a_m2, fill2_ref[:, cs])
        else:
            prev = a_scr[:, cs]
            head1 = jnp.where(row8 == 0, prev[CARRY - 1:CARRY, :], a_m1[0:CARRY, :])
            head2 = jnp.where(row8 == 0, prev[CARRY - 2:CARRY - 1, :],
                              jnp.where(row8 == 1, prev[CARRY - 1:CARRY, :], a_m2[0:CARRY, :]))
            a_m1 = jnp.concatenate([head1, a_m1[CARRY:, :]], axis=0)
            a_m2 = jnp.concatenate([head2, a_m2[CARRY:, :]], axis=0)
            a_scr[:, cs] = a[C - CARRY:, :]
        conv = cb_ref[:, cs] + cw_ref[0:1, cs] * a_m2 + cw_ref[1:2, cs] * a_m1 + cw_ref[2:3, cs] * a
        th = jnp.tanh(conv * (GELU_K0 + GELU_K1 * (conv * conv)))
        pending.append(((conv + conv * th) * gate).astype(BF16))
        if len(pending) == DOWN_GROUP or j == n_chunks - 1:
            lo = (j + 1 - len(pending)) * FF_CHUNK
            hidden = pending[0] if len(pending) == 1 else jnp.concatenate(pending, axis=1)
            part = jnp.dot(hidden, wd_ref[lo:(j + 1) * FF_CHUNK, :], preferred_element_type=F32)
            ffn = part if ffn is None else ffn + part
            pending = []
        if cfg.per_seq:
            a_seq = a.reshape(gs, L, FF_CHUNK)
            cs_ref[0, :, cs] = a_seq[:, Lv - 2, :]
            cs_ref[1, :, cs] = a_seq[:, Lv - 1, :]
        else:
            cs_ref[0, :, cs] = a[Lv - 2:Lv, :]
    y_ref[...] = _ln(ALPHA * x + ffn, l2g_ref[...], l2b_ref[...])


def _const_spec(shape):
    nd = len(shape)
    return pl.BlockSpec(shape, lambda b, t: (0,) * nd, pipeline_mode=pl.Buffered(1))


def _layer_spec(shape, layer):
    nd = len(shape)
    return pl.BlockSpec((None,) + tuple(shape), lambda b, t: (layer,) + (0,) * nd,
                        pipeline_mode=pl.Buffered(1))


def _params():
    return pltpu.CompilerParams(dimension_semantics=("arbitrary", "arbitrary"),
                                vmem_limit_bytes=VMEM_LIMIT_V7X)


def _mix_call(cfg, x, eg, eb, win, wo, lb, hgn, rtg, rtb, l1g, l1b, cos, sin, hg0, rt0, stacked=None):
    C, L, Lv, nb, nt = cfg.C, cfg.L, cfg.Lv, cfg.nb, cfg.nt
    gs = C // L
    a_stack, lv, dec, qdec, kdec, g_total = _tables(C, L, Lv)
    a_stack = jnp.asarray(a_stack, BF16)
    l = cfg.layer
    step_rows = cfg.sub * C
    rows = nb * nt * step_rows
    assert cfg.sub == 1 or gs == 1
    row_spec = pl.BlockSpec((step_rows, D_MODEL), lambda b, t: (b * nt + t, 0))
    if cfg.per_seq:
        st_spec = pl.BlockSpec((None, gs, HEADS, DK, DK), lambda b, t: (l, b, 0, 0, 0))
    else:
        st_spec = pl.BlockSpec((gs, HEADS, DK, DK), lambda b, t: (0, 0, 0, 0))
    in_specs = [
        row_spec,
        _const_spec((1, D_MODEL)), _const_spec((1, D_MODEL)),
        _layer_spec((D_MODEL, PROJ_COLS), l), _layer_spec((D_MODEL, D_MODEL), l),
        _layer_spec((1, HW), l), _layer_spec((1, HW), l), _layer_spec((1, HW), l), _layer_spec((1, HW), l),
        _layer_spec((1, D_MODEL), l), _layer_spec((1, D_MODEL), l),
        pl.BlockSpec((step_rows, HW), lambda b, t: (t, 0)), pl.BlockSpec((step_rows, HW), lambda b, t: (t, 0)),
        _const_spec(a_stack.shape), _const_spec(lv.shape), _const_spec(dec.shape),
        _const_spec(qdec.shape), _const_spec(kdec.shape),
        st_spec, st_spec,
    ]
    operands = [x, eg, eb, win, wo, lb, hgn, rtg, rtb, l1g, l1b, cos, sin,
                a_stack, jnp.asarray(lv), jnp.asarray(dec), jnp.asarray(qdec), jnp.asarray(kdec), hg0, rt0]
    aliases = {}
    if cfg.per_seq:
        out_st = pl.BlockSpec((None, gs, HEADS, DK, DK), lambda b, t: (l, b, 0, 0, 0))
        st_shape = jax.ShapeDtypeStruct((DEPTH, nb * gs, HEADS, DK, DK), F32)
        if stacked is not None:
            aliases = {len(operands): 1, len(operands) + 1: 2}
            in_specs += [pl.BlockSpec(memory_space=pl.ANY), pl.BlockSpec(memory_space=pl.ANY)]
            operands += list(stacked)
    else:
        out_st = pl.BlockSpec((gs, HEADS, DK, DK), lambda b, t: (b, 0, 0, 0))
        st_shape = jax.ShapeDtypeStruct((nb * gs, HEADS, DK, DK), F32)
    return pl.pallas_call(
        functools.partial(_mix_kernel, cfg, g_total),
        out_shape=(jax.ShapeDtypeStruct((rows, D_MODEL), F32), st_shape, st_shape),
        grid=(nb, nt),
        in_specs=in_specs,
        out_specs=(row_spec, out_st, out_st),
        input_output_aliases=aliases,
        compiler_params=_params(),
        name=f"mixer_l{l}_L{L}_nt{nt}",
    )(*operands)


def _ffn_call(cfg, x, wa, wb, cw, cb, wd, l2g, l2b, *conv_in):
    C, L, nb, nt = cfg.C, cfg.L, cfg.nb, cfg.nt
    gs = C // L
    l = cfg.layer
    rows = nb * nt * C
    row_spec = pl.BlockSpec((C, D_MODEL), lambda b, t: (b * nt + t, 0))
    in_specs = [
        row_spec,
        _layer_spec((D_MODEL, D_FF), l), _layer_spec((D_MODEL, D_FF), l),
        _layer_spec((CONV_W, D_FF), l), _layer_spec((1, D_FF), l),
        _layer_spec((D_FF, D_MODEL), l),
        _layer_spec((1, D_MODEL), l), _layer_spec((1, D_MODEL), l),
    ]
    if cfg.per_seq:
        fill_spec = pl.BlockSpec((C, D_FF), lambda b, t: (b, 0))
        in_specs += [fill_spec, fill_spec]
        cs_shape = jax.ShapeDtypeStruct((2, nb * gs, D_FF), F32)
        cs_spec = pl.BlockSpec((2, gs, D_FF), lambda b, t: (0, b, 0))
    else:
        in_specs += [pl.BlockSpec((1, CONV_W - 1, D_FF), lambda b, t: (0, 0, 0))]
        cs_shape = jax.ShapeDtypeStruct((nb, CONV_W - 1, D_FF), F32)
        cs_spec = pl.BlockSpec((1, CONV_W - 1, D_FF), lambda b, t: (b, 0, 0))
    return pl.pallas_call(
        functools.partial(_ffn_kernel, cfg),
        out_shape=(jax.ShapeDtypeStruct((rows, D_MODEL), F32), cs_shape),
        grid=(nb, nt),
        in_specs=in_specs,
        out_specs=(row_spec, cs_spec),
        scratch_shapes=[pltpu.VMEM((CARRY, D_FF), F32)],
        compiler_params=_params(),
        name=f"convffn_l{l}_L{L}_nt{nt}",
    )(x, wa, wb, cw, cb, wd, l2g, l2b, *conv_in)


def _rope_tables(pos):
    half = DK // 2
    inv = ROPE_BASE ** (-jnp.arange(half, dtype=F32) / half)
    ang = pos.astype(F32)[:, None] * inv[None, :]
    cos, sin = jnp.cos(ang), jnp.sin(ang)
    cos2 = jnp.concatenate([cos, cos], axis=1)
    sin2 = jnp.concatenate([-sin, sin], axis=1)
    return jnp.tile(cos2, (1, HEADS)), jnp.tile(sin2, (1, HEADS))


def kernel(x_prompt, x_sample, state_hgrn, state_ret, state_conv, meta_tokens, emb_ln_g, emb_ln_b, lb_logits, w_in, hg_norm_g, rt_gn_g, rt_gn_b, w_o, ln1_g, ln1_b, w_a, w_b, conv_w, conv_b, w_down, ln2_g, ln2_b):
    bsz, seq, _ = x_prompt.shape
    dec_b, dec_t, _ = x_sample.shape
    C = TILE
    assert seq % (C * MIX_SUB) == 0 and seq % FFN_TILE == 0 and N_META <= C and dec_t >= CONV_W - 1
    L_s = 8
    assert dec_t <= L_s and (dec_b * L_s) % C == 0

    sm = jax.nn.softmax(lb_logits.astype(F32), axis=0)
    lb_all = (jnp.cumsum(sm, axis=0) - sm[0:1])[:, None, :]
    win, wo = w_in.astype(BF16), w_o.astype(BF16)
    wa, wb, wd = w_a.astype(BF16), (0.5 * w_b).astype(BF16), w_down.astype(BF16)
    vec = lambda v: v[:, None, :]
    hgn, rtg, rtb = vec(hg_norm_g), vec(rt_gn_g), vec(rt_gn_b)
    l1g, l1b, l2g, l2b, cb = vec(ln1_g), vec(ln1_b), vec(ln2_g), vec(ln2_b), vec(conv_b)
    eg, eb = emb_ln_g[None, :], emb_ln_b[None, :]

    cos_m, sin_m = _rope_tables(jnp.arange(C, dtype=jnp.int32))
    cos_p, sin_p = _rope_tables(N_META + jnp.arange(seq, dtype=jnp.int32))
    cos_s, sin_s = _rope_tables(PAST_LEN + jnp.arange(C, dtype=jnp.int32) % L_s)

    xm = jnp.pad(meta_tokens.astype(F32), ((0, C - N_META), (0, 0)))
    xp = x_prompt.reshape(bsz * seq, D_MODEL)
    xs = jnp.pad(x_sample, ((0, 0), (0, L_s - dec_t), (0, 0))).reshape(dec_b * L_s, D_MODEL)
    zero_state = jnp.zeros((1, HEADS, DK, DK), F32)
    zero_conv = jnp.zeros((1, CONV_W - 1, D_FF), F32)

    hg_p, rt_p, cv_p, cv_s = [], [], [], []
    st_s = (jnp.zeros_like(state_hgrn, dtype=F32), jnp.zeros_like(state_ret, dtype=F32))
    for l in range(DEPTH):
        emb = l == 0
        cfg_m = Cfg(1, 1, C, C, N_META, emb, False, l)
        cfg_p = Cfg(bsz, seq // (C * MIX_SUB), C, C, C, emb, False, l, MIX_SUB)
        cfg_s = Cfg(dec_b * L_s // C, 1, C, L_s, dec_t, emb, True, l)
        mix_w = (eg, eb, win, wo, lb_all, hgn, rtg, rtb, l1g, l1b)
        ffn_w = (wa, wb, conv_w, cb, wd, l2g, l2b)

        xm1, hg_m, rt_m = _mix_call(cfg_m, xm, *mix_w, cos_m, sin_m, zero_state, zero_state)
        xm, cv_m = _ffn_call(cfg_m, xm1, *ffn_w, zero_conv)

        xp1, hg, rt = _mix_call(cfg_p, xp, *mix_w, cos_p, sin_p, hg_m, rt_m)
        cfg_pf = Cfg(bsz, seq // FFN_TILE, FFN_TILE, FFN_TILE, FFN_TILE, False, False, l)
        xp, cv = _ffn_call(cfg_pf, xp1, *ffn_w, cv_m)
        hg_p.append(hg)
        rt_p.append(rt)
        cv_p.append(cv)

        xs1, *st_s = _mix_call(cfg_s, xs, *mix_w, cos_s, sin_s, state_hgrn, state_ret, stacked=st_s)
        sc = state_conv[l]
        fill1 = jnp.pad(sc[:, 1:2], ((0, 0), (0, L_s - 1), (0, 0))).reshape(dec_b * L_s, D_FF)
        fill2 = jnp.pad(sc, ((0, 0), (0, L_s - 2), (0, 0))).reshape(dec_b * L_s, D_FF)
        xs, cv = _ffn_call(cfg_s, xs1, *ffn_w, fill1, fill2)
        cv_s.append(jnp.swapaxes(cv, 0, 1))

    y_prompt = xp.reshape(bsz, seq, D_MODEL)
    y_sample = xs.reshape(dec_b, L_s, D_MODEL)[:, :dec_t]
    hg_s, rt_s = st_s
    return (y_prompt, y_sample, jnp.stack(hg_p), hg_s, jnp.stack(rt_p), rt_s, jnp.stack(cv_p), jnp.stack(cv_s))
```

```python
import functools
from typing import NamedTuple

import numpy as np
import jax
import jax.numpy as jnp
from jax import lax
from jax.experimental import pallas as pl
from jax.experimental.pallas import tpu as pltpu

D_MODEL = 1024
DEPTH = 4
HEADS = 4
DK = 128
HW = HEADS * DK
PROJ_COLS = 8 * HW
N_META = 16
PAST_LEN = 16384
CONV_W = 3
D_FF = 2816
ROPE_BASE = 10000.0
LN_EPS = 1e-5
F_EPS = 1e-6
ALPHA = (2 * DEPTH) ** 0.25

TILE = 128
MIX_SUB = 4
N_PARTS = PROJ_COLS // HW
FFN_TILE = 1024
FF_CHUNK = 256
DOWN_GROUP = 4
GELU_K0 = float((2.0 / np.pi) ** 0.5)
GELU_K1 = 0.044715 * GELU_K0
CARRY = 8
VMEM_LIMIT_V7X = 56 * 1024 * 1024

F32 = jnp.float32
BF16 = jnp.bfloat16
NT_DIMS = (((1,), (1,)), ((), ()))
TN_DIMS = (((0,), (0,)), ((), ()))


class Cfg(NamedTuple):
    nb: int
    nt: int
    C: int
    L: int
    Lv: int
    embed: bool
    per_seq: bool
    layer: int
    sub: int = 1


def _levels(Lv):
    return [m for m in (1, 2, 4, 8, 16, 32, 64, 128, 256) if m < Lv]


@functools.lru_cache(maxsize=None)
def _tables(C, L, Lv):
    r = np.arange(C)
    g, p = r // L, r % L
    same = g[:, None] == g[None, :]
    pt, pu = p[:, None], p[None, :]
    blocks = [same & (pu <= pt), same & (pu > pt)]
    lv = np.full((C, C), -1, np.int32)
    ok = same & (pt < Lv) & (pu < Lv)
    lv[ok & (pt == pu)] = 0
    for j, m in enumerate(_levels(Lv)):
        blk_t, blk_u = pt // (2 * m), pu // (2 * m)
        half_t, half_u = (pt // m) % 2, (pu // m) % 2
        lv[ok & same & (blk_t == blk_u) & (half_t == 1) & (half_u == 0)] = j + 1
    a_stack = np.concatenate(blocks, 0).astype(np.float32)
    a_stack = np.concatenate([a_stack, a_stack], 1)

    log_gamma = np.log1p(-np.exp2(-5.0 - np.arange(HEADS, dtype=np.float64)))
    rel = (pt - pu).astype(np.float64)
    causal = same & (pu <= pt) & (pu < Lv)
    dec = np.where(causal[None], np.exp(np.where(causal, rel, 0.0)[None] * log_gamma[:, None, None]), 0.0)
    qdec = np.exp((p + 1.0)[:, None] * log_gamma[None, :])
    kdec = np.where((p < Lv)[:, None], np.exp((Lv - 1.0 - p)[:, None] * log_gamma[None, :]), 0.0)
    qdec = np.repeat(qdec, DK, axis=1)
    kdec = np.repeat(kdec, DK, axis=1)
    g_total = tuple(float(x) for x in np.exp(Lv * log_gamma))
    return (a_stack, lv, dec.astype(np.float32), qdec.astype(np.float32), kdec.astype(np.float32), g_total)


def _ln(x, g, b):
    mu = jnp.mean(x, -1, keepdims=True)
    xc = x - mu
    var = jnp.mean(xc * xc, -1, keepdims=True)
    return xc * lax.rsqrt(var + LN_EPS) * g + b


def _silu(x):
    h = 0.5 * x
    return h + h * jnp.tanh(h)


def _expand(x, gs, mask):
    if gs == 1:
        return x
    return jnp.where(mask, jnp.concatenate([x] * gs, axis=1), jnp.zeros((), x.dtype))


def _col_bcast(rows, gs):
    outs = []
    for g in range(gs):
        outs.append(jnp.broadcast_to(rows[g:g + 1, :], (DK, DK)).T)
    return outs[0] if gs == 1 else jnp.concatenate(outs, axis=0)


def _boundary_rows(b, m):
    C, W = b.shape
    blk = 2 * m
    if blk >= 8:
        b3 = b.reshape(C // blk, blk, W)
        return jnp.broadcast_to(b3[:, m - 1:m, :], b3.shape).reshape(C, W)
    b3 = b.reshape(C // 8, 8, W)
    sub = lax.broadcasted_iota(jnp.int32, (1, 8, 1), 1)
    r = jnp.broadcast_to(b3[:, m - 1:m, :], b3.shape)
    for k in range(1, 8 // blk):
        r = jnp.where(sub >= k * blk, jnp.broadcast_to(b3[:, k * blk + m - 1:k * blk + m, :], b3.shape), r)
    return r.reshape(C, W)


def _mix_kernel(cfg, g_total, x_ref, eg_ref, eb_ref, win_ref, wo_ref, lb_ref, hgn_ref, rtg_ref, rtb_ref,
                l1g_ref, l1b_ref, cos_ref, sin_ref, a_ref, lv_ref, dec_ref, qdec_ref, kdec_ref,
                hg0_ref, rt0_ref, *rest):
    x1_ref, hg_ref, rt_ref = rest[-3:]
    C = cfg.C
    t = pl.program_id(1)

    @pl.when(t == 0)
    def _():
        hg_ref[...] = hg0_ref[...]
        rt_ref[...] = rt0_ref[...]

    x = x_ref[...]
    if cfg.embed:
        x = _ln(x, eg_ref[...], eb_ref[...])
    xb = x.astype(BF16)
    parts = [jnp.dot(xb, win_ref[:, k * HW:(k + 1) * HW], preferred_element_type=F32) for k in range(N_PARTS)]
    outs = []
    for s in range(cfg.sub):
        rows = slice(s * C, (s + 1) * C)
        outs.append(_mix_tile(cfg, g_total, [p[rows, :] for p in parts], cos_ref[rows, :], sin_ref[rows, :],
                              lb_ref, hgn_ref, rtg_ref, rtb_ref, a_ref, lv_ref, dec_ref, qdec_ref, kdec_ref,
                              hg_ref, rt_ref))
    o_all = outs[0] if cfg.sub == 1 else jnp.concatenate(outs, axis=0)
    mix = jnp.dot(o_all, wo_ref[...], preferred_element_type=F32)
    x1_ref[...] = _ln(ALPHA * x + mix, l1g_ref[...], l1b_ref[...])


def _mix_tile(cfg, g_total, parts, cos, sin, lb_ref, hgn_ref, rtg_ref, rtb_ref, a_ref, lv_ref, dec_ref,
              qdec_ref, kdec_ref, hg_ref, rt_ref):
    C, L, Lv = cfg.C, cfg.L, cfg.Lv
    gs = C // L
    levels = _levels(Lv)
    hq, hf, hv, hgate, rq, rk, rv, rgate = parts
    heads = [slice(h * DK, (h + 1) * DK) for h in range(HEADS)]

    if gs > 1:
        row_seq = lax.broadcasted_iota(jnp.int32, (C, gs * DK), 0) // L
        lane_blk = lax.broadcasted_iota(jnp.int32, (C, gs * DK), 1) // DK
        seq_mask = row_seq == lane_blk
    else:
        seq_mask = None
    lv = lv_ref[...]


    qdec, kdec = qdec_ref[...], kdec_ref[...]
    rt_q, rt_k, rt_sc = [], [], []
    for h, sl in enumerate(heads):
        qh = rq[:, sl] * cos + pltpu.roll(rq[:, sl], DK // 2, 1) * sin
        kh = (rk[:, sl] * cos + pltpu.roll(rk[:, sl], DK // 2, 1) * sin) * (DK ** -0.5)
        sc = lax.dot_general(qh.astype(BF16), kh.astype(BF16), NT_DIMS, preferred_element_type=F32)
        rt_q.append(qh)
        rt_k.append(kh)
        rt_sc.append((sc * dec_ref[h]).astype(BF16))

    lb = lb_ref[...]
    th = jnp.tanh(0.5 * hf)
    f_gate = lb + (1.0 - lb) * (0.5 + 0.5 * th)
    lf = jnp.log(jnp.maximum(f_gate, F_EPS))
    kx = (1.0 - lb) * (0.5 - 0.5 * th)
    q = _silu(hq)
    if Lv < L:
        real = (lax.broadcasted_iota(jnp.int32, (C, 1), 0) % L) < Lv
        lf = jnp.where(real, lf, 0.0)
        kx = jnp.where(real, kx, 0.0)
    lf_hi = lf.astype(BF16)
    rem = lf - lf_hi.astype(F32)
    lf_mid = rem.astype(BF16)
    lf_lo = (rem - lf_mid.astype(F32)).astype(BF16)
    sums = (jnp.dot(a_ref[...], jnp.concatenate([lf_hi, lf_mid], axis=0), preferred_element_type=F32)
            + jnp.dot(a_ref[:, 0:C], lf_lo, preferred_element_type=F32))
    b_incl = sums[0:C]
    q_in = q * jnp.exp(b_incl)
    k_out = kx * jnp.exp(sums[C:2 * C])
    if gs == 1:
        b_last = b_incl[Lv - 1:Lv, :]
    else:
        b_last = b_incl.reshape(gs, L, HW)[:, Lv - 1, :]
    d_rows = jnp.exp(b_last)

    qb, kb = q.astype(BF16), kx.astype(BF16)
    level_sc = [[lax.dot_general(qb[:, sl], kb[:, sl], NT_DIMS, preferred_element_type=F32)] for sl in heads]
    for m in levels:
        e = jnp.exp(-jnp.abs(b_incl - _boundary_rows(b_incl, m))).astype(BF16)
        qe, ke = qb * e, kb * e
        for h, sl in enumerate(heads):
            level_sc[h].append(lax.dot_general(qe[:, sl], ke[:, sl], NT_DIMS, preferred_element_type=F32))

    outs_rt = []
    for h, sl in enumerate(heads):
        vb = rv[:, sl].astype(BF16)
        s_old = rt_ref[:, h].reshape(gs * DK, DK)
        q_st = _expand((rt_q[h] * qdec[:, sl]).astype(BF16), gs, seq_mask)
        o = jnp.dot(jnp.concatenate([rt_sc[h], q_st], axis=1),
                    jnp.concatenate([vb, s_old.astype(BF16)], axis=0), preferred_element_type=F32)
        upd = lax.dot_general(_expand((rt_k[h] * kdec[:, sl]).astype(BF16), gs, seq_mask), vb, TN_DIMS,
                              preferred_element_type=F32)
        rt_ref[:, h] = (g_total[h] * s_old + upd).reshape(gs, DK, DK)
        mu = jnp.mean(o, -1, keepdims=True)
        oc = o - mu
        o = oc * lax.rsqrt(jnp.mean(oc * oc, -1, keepdims=True) + LN_EPS)
        outs_rt.append((o * rtg_ref[:, sl] + rtb_ref[:, sl]) * _silu(rgate[:, sl]))

    outs_hg = []
    for h, sl in enumerate(heads):
        vb = hv[:, sl].astype(BF16)
        s_old = hg_ref[:, h].reshape(gs * DK, DK)
        q_st = _expand(q_in[:, sl].astype(BF16), gs, seq_mask)
        scores = jnp.where(lv == 0, level_sc[h][0], 0.0)
        for j in range(len(levels)):
            scores = jnp.where(lv == j + 1, level_sc[h][j + 1], scores)
        o = jnp.dot(jnp.concatenate([scores.astype(BF16), q_st], axis=1),
                    jnp.concatenate([vb, s_old.astype(BF16)], axis=0), preferred_element_type=F32)
        upd = lax.dot_general(_expand(k_out[:, sl].astype(BF16), gs, seq_mask), vb, TN_DIMS,
                              preferred_element_type=F32)
        hg_ref[:, h] = (_col_bcast(d_rows[:, sl], gs) * s_old + upd).reshape(gs, DK, DK)
        o = o * lax.rsqrt(jnp.mean(o * o, -1, keepdims=True) + LN_EPS)
        outs_hg.append(o * hgn_ref[:, sl] * _silu(hgate[:, sl]))

    return jnp.concatenate(outs_hg + outs_rt, axis=1).astype(BF16)


def _ffn_kernel(cfg, x_ref, wa_ref, wb_ref, cw_ref, cb_ref, wd_ref, l2g_ref, l2b_ref, *rest):
    C, L, Lv = cfg.C, cfg.L, cfg.Lv
    gs = C // L
    if cfg.per_seq:
        fill1_ref, fill2_ref, y_ref, cs_ref, a_scr = rest
    else:
        init_ref, y_ref, cs_ref, a_scr = rest
    t = pl.program_id(1)

    x = x_ref[...]
    xb = x.astype(BF16)
    if cfg.per_seq:
        pos = lax.broadcasted_iota(jnp.int32, (C, 1), 0) % L
    else:
        @pl.when(t == 0)
        def _():
            a_scr[...] = jnp.zeros((CARRY, D_FF), F32)
            a_scr[CARRY - 2:CARRY, :] = init_ref[0]
        row8 = lax.broadcasted_iota(jnp.int32, (CARRY, 1), 0)

    def up_proj(j):
        cs = slice(j * FF_CHUNK, (j + 1) * FF_CHUNK)
        return (jnp.dot(xb, wa_ref[:, cs], preferred_element_type=F32),
                jnp.dot(xb, wb_ref[:, cs], preferred_element_type=F32))

    n_chunks = D_FF // FF_CHUNK
    ffn = None
    pending = []
    ahead = up_proj(0)
    for j in range(n_chunks):
        cs = slice(j * FF_CHUNK, (j + 1) * FF_CHUNK)
        a, gate = ahead
        if j + 1 < n_chunks:
            ahead = up_proj(j + 1)
        a_m1 = pltpu.roll(a, 1, 0)
        a_m2 = pltpu.roll(a, 2, 0)
        if cfg.per_seq:
            a_m1 = jnp.where(pos >= 1, a_m1, fill1_ref[:, cs])
            a_m2 = jnp.where(pos >= 2, a_m2, fill2_ref[:, cs])
        else:
            prev = a_scr[:, cs]
            head1 = jnp.where(row8 == 0, prev[CARRY - 1:CARRY, :], a_m1[0:CARRY, :])
            head2 = jnp.where(row8 == 0, prev[CARRY - 2:CARRY - 1, :],
                              jnp.where(row8 == 1, prev[CARRY - 1:CARRY, :], a_m2[0:CARRY, :]))
            a_m1 = jnp.concatenate([head1, a_m1[CARRY:, :]], axis=0)
            a_m2 = jnp.concatenate([head2, a_m2[CARRY:, :]], axis=0)
            a_scr[:, cs] = a[C - CARRY:, :]
        conv = cb_ref[:, cs] + cw_ref[0:1, cs] * a_m2 + cw_ref[1:2, cs] * a_m1 + cw_ref[2:3, cs] * a
        th = jnp.tanh(conv * (GELU_K0 + GELU_K1 * (conv * conv)))
        pending.append(((conv + conv * th) * gate).astype(BF16))
        if len(pending) == DOWN_GROUP or j == n_chunks - 1:
            lo = (j + 1 - len(pending)) * FF_CHUNK
            hidden = pending[0] if len(pending) == 1 else jnp.concatenate(pending, axis=1)
            part = jnp.dot(hidden, wd_ref[lo:(j + 1) * FF_CHUNK, :], preferred_element_type=F32)
            ffn = part if ffn is None else ffn + part
            pending = []
        if cfg.per_seq:
            a_seq = a.reshape(gs, L, FF_CHUNK)
            cs_ref[0, :, cs] = a_seq[:, Lv - 2, :]
            cs_ref[1, :, cs] = a_seq[:, Lv - 1, :]
        else:
            cs_ref[0, :, cs] = a[Lv - 2:Lv, :]
    y_ref[...] = _ln(ALPHA * x + ffn, l2g_ref[...], l2b_ref[...])


def _const_spec(shape):
    nd = len(shape)
    return pl.BlockSpec(shape, lambda b, t: (0,) * nd, pipeline_mode=pl.Buffered(1))


def _layer_spec(shape, layer):
    nd = len(shape)
    return pl.BlockSpec((None,) + tuple(shape), lambda b, t: (layer,) + (0,) * nd,
                        pipeline_mode=pl.Buffered(1))


def _params():
    return pltpu.CompilerParams(dimension_semantics=("arbitrary", "arbitrary"),
                                vmem_limit_bytes=VMEM_LIMIT_V7X)


def _mix_call(cfg, x, eg, eb, win, wo, lb, hgn, rtg, rtb, l1g, l1b, cos, sin, hg0, rt0, stacked=None):
    C, L, Lv, nb, nt = cfg.C, cfg.L, cfg.Lv, cfg.nb, cfg.nt
    gs = C // L
    a_stack, lv, dec, qdec, kdec, g_total = _tables(C, L, Lv)
    a_stack = jnp.asarray(a_stack, BF16)
    l = cfg.layer
    step_rows = cfg.sub * C
    rows = nb * nt * step_rows
    assert cfg.sub == 1 or gs == 1
    row_spec = pl.BlockSpec((step_rows, D_MODEL), lambda b, t: (b * nt + t, 0))
    if cfg.per_seq:
        st_spec = pl.BlockSpec((None, gs, HEADS, DK, DK), lambda b, t: (l, b, 0, 0, 0))
    else:
        st_spec = pl.BlockSpec((gs, HEADS, DK, DK), lambda b, t: (0, 0, 0, 0))
    in_specs = [
        row_spec,
        _const_spec((1, D_MODEL)), _const_spec((1, D_MODEL)),
        _layer_spec((D_MODEL, PROJ_COLS), l), _layer_spec((D_MODEL, D_MODEL), l),
        _layer_spec((1, HW), l), _layer_spec((1, HW), l), _layer_spec((1, HW), l), _layer_spec((1, HW), l),
        _layer_spec((1, D_MODEL), l), _layer_spec((1, D_MODEL), l),
        pl.BlockSpec((step_rows, DK), lambda b, t: (t, 0)), pl.BlockSpec((step_rows, DK), lambda b, t: (t, 0)),
        _const_spec(a_stack.shape), _const_spec(lv.shape), _const_spec(dec.shape),
        _const_spec(qdec.shape), _const_spec(kdec.shape),
        st_spec, st_spec,
    ]
    operands = [x, eg, eb, win, wo, lb, hgn, rtg, rtb, l1g, l1b, cos, sin,
                a_stack, jnp.asarray(lv), jnp.asarray(dec), jnp.asarray(qdec), jnp.asarray(kdec), hg0, rt0]
    aliases = {}
    if cfg.per_seq:
        out_st = pl.BlockSpec((None, gs, HEADS, DK, DK), lambda b, t: (l, b, 0, 0, 0))
        st_shape = jax.ShapeDtypeStruct((DEPTH, nb * gs, HEADS, DK, DK), F32)
        if stacked is not None:
            aliases = {len(operands): 1, len(operands) + 1: 2}
            in_specs += [pl.BlockSpec(memory_space=pl.ANY), pl.BlockSpec(memory_space=pl.ANY)]
            operands += list(stacked)
    else:
        out_st = pl.BlockSpec((gs, HEADS, DK, DK), lambda b, t: (b, 0, 0, 0))
        st_shape = jax.ShapeDtypeStruct((nb * gs, HEADS, DK, DK), F32)
    return pl.pallas_call(
        functools.partial(_mix_kernel, cfg, g_total),
        out_shape=(jax.ShapeDtypeStruct((rows, D_MODEL), F32), st_shape, st_shape),
        grid=(nb, nt),
        in_specs=in_specs,
        out_specs=(row_spec, out_st, out_st),
        input_output_aliases=aliases,
        compiler_params=_params(),
        name=f"mixer_l{l}_L{L}_nt{nt}",
    )(*operands)


def _ffn_call(cfg, x, wa, wb, cw, cb, wd, l2g, l2b, *conv_in):
    C, L, nb, nt = cfg.C, cfg.L, cfg.nb, cfg.nt
    gs = C // L
    l = cfg.layer
    rows = nb * nt * C
    row_spec = pl.BlockSpec((C, D_MODEL), lambda b, t: (b * nt + t, 0))
    in_specs = [
        row_spec,
        _layer_spec((D_MODEL, D_FF), l), _layer_spec((D_MODEL, D_FF), l),
        _layer_spec((CONV_W, D_FF), l), _layer_spec((1, D_FF), l),
        _layer_spec((D_FF, D_MODEL), l),
        _layer_spec((1, D_MODEL), l), _layer_spec((1, D_MODEL), l),
    ]
    if cfg.per_seq:
        fill_spec = pl.BlockSpec((C, D_FF), lambda b, t: (b, 0))
        in_specs += [fill_spec, fill_spec]
        cs_shape = jax.ShapeDtypeStruct((2, nb * gs, D_FF), F32)
        cs_spec = pl.BlockSpec((2, gs, D_FF), lambda b, t: (0, b, 0))
    else:
        in_specs += [pl.BlockSpec((1, CONV_W - 1, D_FF), lambda b, t: (0, 0, 0))]
        cs_shape = jax.ShapeDtypeStruct((nb, CONV_W - 1, D_FF), F32)
        cs_spec = pl.BlockSpec((1, CONV_W - 1, D_FF), lambda b, t: (b, 0, 0))
    return pl.pallas_call(
        functools.partial(_ffn_kernel, cfg),
        out_shape=(jax.ShapeDtypeStruct((rows, D_MODEL), F32), cs_shape),
        grid=(nb, nt),
        in_specs=in_specs,
        out_specs=(row_spec, cs_spec),
        scratch_shapes=[pltpu.VMEM((CARRY, D_FF), F32)],
        compiler_params=_params(),
        name=f"convffn_l{l}_L{L}_nt{nt}",
    )(x, wa, wb, cw, cb, wd, l2g, l2b, *conv_in)


def _rope_tables(pos):
    half = DK // 2
    inv = ROPE_BASE ** (-jnp.arange(half, dtype=F32) / half)
    ang = pos.astype(F32)[:, None] * inv[None, :]
    cos, sin = jnp.cos(ang), jnp.sin(ang)
    cos2 = jnp.concatenate([cos, cos], axis=1)
    sin2 = jnp.concatenate([-sin, sin], axis=1)
    return cos2, sin2


def kernel(x_prompt, x_sample, state_hgrn, state_ret, state_conv, meta_tokens, emb_ln_g, emb_ln_b, lb_logits, w_in, hg_norm_g, rt_gn_g, rt_gn_b, w_o, ln1_g, ln1_b, w_a, w_b, conv_w, conv_b, w_down, ln2_g, ln2_b):
    bsz, seq, _ = x_prompt.shape
    dec_b, dec_t, _ = x_sample.shape
    C = TILE
    assert seq % (C * MIX_SUB) == 0 and seq % FFN_TILE == 0 and N_META <= C and dec_t >= CONV_W - 1
    L_s = 8
    assert dec_t <= L_s and (dec_b * L_s) % C == 0

    sm = jax.nn.softmax(lb_logits.astype(F32), axis=0)
    lb_all = (jnp.cumsum(sm, axis=0) - sm[0:1])[:, None, :]
    win, wo = w_in.astype(BF16), w_o.astype(BF16)
    wa, wb, wd = w_a.astype(BF16), (0.5 * w_b).astype(BF16), w_down.astype(BF16)
    vec = lambda v: v[:, None, :]
    hgn, rtg, rtb = vec(hg_norm_g), vec(rt_gn_g), vec(rt_gn_b)
    l1g, l1b, l2g, l2b, cb = vec(ln1_g), vec(ln1_b), vec(ln2_g), vec(ln2_b), vec(conv_b)
    eg, eb = emb_ln_g[None, :], emb_ln_b[None, :]

    cos_m, sin_m = _rope_tables(jnp.arange(C, dtype=jnp.int32))
    cos_p, sin_p = _rope_tables(N_META + jnp.arange(seq, dtype=jnp.int32))
    cos_s, sin_s = _rope_tables(PAST_LEN + jnp.arange(C, dtype=jnp.int32) % L_s)

    xm = jnp.pad(meta_tokens.astype(F32), ((0, C - N_META), (0, 0)))
    xp = x_prompt.reshape(bsz * seq, D_MODEL)
    xs = jnp.pad(x_sample, ((0, 0), (0, L_s - dec_t), (0, 0))).reshape(dec_b * L_s, D_MODEL)
    zero_state = jnp.zeros((1, HEADS, DK, DK), F32)
    zero_conv = jnp.zeros((1, CONV_W - 1, D_FF), F32)

    hg_p, rt_p, cv_p, cv_s = [], [], [], []
    st_s = (jnp.zeros_like(state_hgrn, dtype=F32), jnp.zeros_like(state_ret, dtype=F32))
    for l in range(DEPTH):
        emb = l == 0
        cfg_m = Cfg(1, 1, C, C, N_META, emb, False, l)
        cfg_p = Cfg(bsz, seq // (C * MIX_SUB), C, C, C, emb, False, l, MIX_SUB)
        cfg_s = Cfg(dec_b * L_s // C, 1, C, L_s, dec_t, emb, True, l)
        mix_w = (eg, eb, win, wo, lb_all, hgn, rtg, rtb, l1g, l1b)
        ffn_w = (wa, wb, conv_w, cb, wd, l2g, l2b)

        xm1, hg_m, rt_m = _mix_call(cfg_m, xm, *mix_w, cos_m, sin_m, zero_state, zero_state)
        xm, cv_m = _ffn_call(cfg_m, xm1, *ffn_w, zero_conv)

        xp1, hg, rt = _mix_call(cfg_p, xp, *mix_w, cos_p, sin_p, hg_m, rt_m)
        cfg_pf = Cfg(bsz, seq // FFN_TILE, FFN_TILE, FFN_TILE, FFN_TILE, False, False, l)
        xp, cv = _ffn_call(cfg_pf, xp1, *ffn_w, cv_m)
        hg_p.append(hg)
        rt_p.append(rt)
        cv_p.append(cv)

        xs1, *st_s = _mix_call(cfg_s, xs, *mix_w, cos_s, sin_s, state_hgrn, state_ret, stacked=st_s)
        sc = state_conv[l]
        fill1 = jnp.pad(sc[:, 1:2], ((0, 0), (0, L_s - 1), (0, 0))).reshape(dec_b * L_s, D_FF)
        fill2 = jnp.pad(sc, ((0, 0), (0, L_s - 2), (0, 0))).reshape(dec_b * L_s, D_FF)
        xs, cv = _ffn_call(cfg_s, xs1, *ffn_w, fill1, fill2)
        cv_s.append(jnp.swapaxes(cv, 0, 1))

    y_prompt = xp.reshape(bsz, seq, D_MODEL)
    y_sample = xs.reshape(dec_b, L_s, D_MODEL)[:, :dec_t]
    hg_s, rt_s = st_s
    return (y_prompt, y_sample, jnp.stack(hg_p), hg_s, jnp.stack(rt_p), rt_s, jnp.stack(cv_p), jnp.stack(cv_s))
```

```python
import functools
from typing import NamedTuple

import numpy as np
import jax
import jax.numpy as jnp
from jax import lax
from jax.experimental import pallas as pl
from jax.experimental.pallas import tpu as pltpu

D_MODEL = 1024
DEPTH = 4
HEADS = 4
DK = 128
HW = HEADS * DK
PROJ_COLS = 8 * HW
N_META = 16
PAST_LEN = 16384
CONV_W = 3
D_FF = 2816
ROPE_BASE = 10000.0
LN_EPS = 1e-5
F_EPS = 1e-6
ALPHA = (2 * DEPTH) ** 0.25

TILE = 128
MIX_SUB = 4
N_PARTS = PROJ_COLS // HW
FFN_TILE = 1024
FFN_TILE_SAMPLE = 256
FF_CHUNK = 256
DOWN_GROUP = 4
GELU_K0 = float((2.0 / np.pi) ** 0.5)
GELU_K1 = 0.044715 * GELU_K0
CARRY = 8
VMEM_LIMIT_V7X = 56 * 1024 * 1024

F32 = jnp.float32
BF16 = jnp.bfloat16
NT_DIMS = (((1,), (1,)), ((), ()))
TN_DIMS = (((0,), (0,)), ((), ()))


class Cfg(NamedTuple):
    nb: int
    nt: int
    C: int
    L: int
    Lv: int
    embed: bool
    per_seq: bool
    layer: int
    sub: int = 1


def _levels(Lv):
    return [m for m in (1, 2, 4, 8, 16, 32, 64, 128, 256) if m < Lv]


@functools.lru_cache(maxsize=None)
def _tables(C, L, Lv):
    r = np.arange(C)
    g, p = r // L, r % L
    same = g[:, None] == g[None, :]
    pt, pu = p[:, None], p[None, :]
    blocks = [same & (pu <= pt), same & (pu > pt)]
    lv = np.full((C, C), -1, np.int32)
    ok = same & (pt < Lv) & (pu < Lv)
    lv[ok & (pt == pu)] = 0
    for j, m in enumerate(_levels(Lv)):
        blk_t, blk_u = pt // (2 * m), pu // (2 * m)
        half_t, half_u = (pt // m) % 2, (pu // m) % 2
        lv[ok & same & (blk_t == blk_u) & (half_t == 1) & (half_u == 0)] = j + 1
    a_stack = np.concatenate(blocks, 0).astype(np.float32)
    a_stack = np.concatenate([a_stack, a_stack], 1)

    log_gamma = np.log1p(-np.exp2(-5.0 - np.arange(HEADS, dtype=np.float64)))
    rel = (pt - pu).astype(np.float64)
    causal = same & (pu <= pt) & (pu < Lv)
    dec = np.where(causal[None], np.exp(np.where(causal, rel, 0.0)[None] * log_gamma[:, None, None]), 0.0)
    qdec = np.exp((p + 1.0)[:, None] * log_gamma[None, :])
    kdec = np.where((p < Lv)[:, None], np.exp((Lv - 1.0 - p)[:, None] * log_gamma[None, :]), 0.0)
    qdec = np.repeat(qdec, DK, axis=1)
    kdec = np.repeat(kdec, DK, axis=1)
    g_total = tuple(float(x) for x in np.exp(Lv * log_gamma))
    return (a_stack, lv, dec.astype(np.float32), qdec.astype(np.float32), kdec.astype(np.float32), g_total)


def _ln(x, g, b):
    mu = jnp.mean(x, -1, keepdims=True)
    xc = x - mu
    var = jnp.mean(xc * xc, -1, keepdims=True)
    return xc * lax.rsqrt(var + LN_EPS) * g + b


def _silu(x):
    h = 0.5 * x
    return h + h * jnp.tanh(h)


def _expand(x, gs, mask):
    if gs == 1:
        return x
    return jnp.where(mask, jnp.concatenate([x] * gs, axis=1), jnp.zeros((), x.dtype))


def _col_bcast(rows, gs):
    outs = []
    for g in range(gs):
        outs.append(jnp.broadcast_to(rows[g:g + 1, :], (DK, DK)).T)
    return outs[0] if gs == 1 else jnp.concatenate(outs, axis=0)


def _boundary_rows(b, m):
    C, W = b.shape
    blk = 2 * m
    if blk >= 8:
        b3 = b.reshape(C // blk, blk, W)
        return jnp.broadcast_to(b3[:, m - 1:m, :], b3.shape).reshape(C, W)
    b3 = b.reshape(C // 8, 8, W)
    sub = lax.broadcasted_iota(jnp.int32, (1, 8, 1), 1)
    r = jnp.broadcast_to(b3[:, m - 1:m, :], b3.shape)
    for k in range(1, 8 // blk):
        r = jnp.where(sub >= k * blk, jnp.broadcast_to(b3[:, k * blk + m - 1:k * blk + m, :], b3.shape), r)
    return r.reshape(C, W)


def _mix_kernel(cfg, g_total, x_ref, eg_ref, eb_ref, win_ref, wo_ref, lb_ref, hgn_ref, rtg_ref, rtb_ref,
                l1g_ref, l1b_ref, cos_ref, sin_ref, a_ref, lv_ref, dec_ref, qdec_ref, kdec_ref,
                hg0_ref, rt0_ref, *rest):
    x1_ref, hg_ref, rt_ref = rest[-3:]
    C = cfg.C
    t = pl.program_id(1)

    @pl.when(t == 0)
    def _():
        hg_ref[...] = hg0_ref[...]
        rt_ref[...] = rt0_ref[...]

    x = x_ref[...]
    if cfg.embed:
        x = _ln(x, eg_ref[...], eb_ref[...])
    xb = x.astype(BF16)
    parts = [jnp.dot(xb, win_ref[:, k * HW:(k + 1) * HW], preferred_element_type=F32) for k in range(N_PARTS)]
    outs = []
    for s in range(cfg.sub):
        rows = slice(s * C, (s + 1) * C)
        outs.append(_mix_tile(cfg, g_total, [p[rows, :] for p in parts], cos_ref[rows, :], sin_ref[rows, :],
                              lb_ref, hgn_ref, rtg_ref, rtb_ref, a_ref, lv_ref, dec_ref, qdec_ref, kdec_ref,
                              hg_ref, rt_ref))
    o_all = outs[0] if cfg.sub == 1 else jnp.concatenate(outs, axis=0)
    mix = jnp.dot(o_all, wo_ref[...], preferred_element_type=F32)
    x1_ref[...] = _ln(ALPHA * x + mix, l1g_ref[...], l1b_ref[...])


def _mix_tile(cfg, g_total, parts, cos, sin, lb_ref, hgn_ref, rtg_ref, rtb_ref, a_ref, lv_ref, dec_ref,
              qdec_ref, kdec_ref, hg_ref, rt_ref):
    C, L, Lv = cfg.C, cfg.L, cfg.Lv
    gs = C // L
    levels = _levels(Lv)
    hq, hf, hv, hgate, rq, rk, rv, rgate = parts
    heads = [slice(h * DK, (h + 1) * DK) for h in range(HEADS)]

    if gs > 1:
        row_seq = lax.broadcasted_iota(jnp.int32, (C, gs * DK), 0) // L
        lane_blk = lax.broadcasted_iota(jnp.int32, (C, gs * DK), 1) // DK
        seq_mask = row_seq == lane_blk
    else:
        seq_mask = None
    lv = lv_ref[...]


    qdec, kdec = qdec_ref[...], kdec_ref[...]
    rt_q, rt_k, rt_sc = [], [], []
    for h, sl in enumerate(heads):
        qh = rq[:, sl] * cos + pltpu.roll(rq[:, sl], DK // 2, 1) * sin
        kh = (rk[:, sl] * cos + pltpu.roll(rk[:, sl], DK // 2, 1) * sin) * (DK ** -0.5)
        sc = lax.dot_general(qh.astype(BF16), kh.astype(BF16), NT_DIMS, preferred_element_type=F32)
        rt_q.append(qh)
        rt_k.append(kh)
        rt_sc.append((sc * dec_ref[h]).astype(BF16))

    lb = lb_ref[...]
    th = jnp.tanh(0.5 * hf)
    f_gate = lb + (1.0 - lb) * (0.5 + 0.5 * th)
    lf = jnp.log(jnp.maximum(f_gate, F_EPS))
    kx = (1.0 - lb) * (0.5 - 0.5 * th)
    q = _silu(hq)
    if Lv < L:
        real = (lax.broadcasted_iota(jnp.int32, (C, 1), 0) % L) < Lv
        lf = jnp.where(real, lf, 0.0)
        kx = jnp.where(real, kx, 0.0)
    lf_hi = lf.astype(BF16)
    rem = lf - lf_hi.astype(F32)
    lf_mid = rem.astype(BF16)
    lf_lo = (rem - lf_mid.astype(F32)).astype(BF16)
    sums = (jnp.dot(a_ref[...], jnp.concatenate([lf_hi, lf_mid], axis=0), preferred_element_type=F32)
            + jnp.dot(a_ref[:, 0:C], lf_lo, preferred_element_type=F32))
    b_incl = sums[0:C]
    q_in = q * jnp.exp(b_incl)
    k_out = kx * jnp.exp(sums[C:2 * C])
    if gs == 1:
        b_last = b_incl[Lv - 1:Lv, :]
    else:
        b_last = b_incl.reshape(gs, L, HW)[:, Lv - 1, :]
    d_rows = jnp.exp(b_last)

    qb, kb = q.astype(BF16), kx.astype(BF16)
    level_sc = [[lax.dot_general(qb[:, sl], kb[:, sl], NT_DIMS, preferred_element_type=F32)] for sl in heads]
    for m in levels:
        e = jnp.exp(-jnp.abs(b_incl - _boundary_rows(b_incl, m))).astype(BF16)
        qe, ke = qb * e, kb * e
        for h, sl in enumerate(heads):
            level_sc[h].append(lax.dot_general(qe[:, sl], ke[:, sl], NT_DIMS, preferred_element_type=F32))

    outs_rt = []
    for h, sl in enumerate(heads):
        vb = rv[:, sl].astype(BF16)
        s_old = rt_ref[:, h].reshape(gs * DK, DK)
        q_st = _expand((rt_q[h] * qdec[:, sl]).astype(BF16), gs, seq_mask)
        o = jnp.dot(jnp.concatenate([rt_sc[h], q_st], axis=1),
                    jnp.concatenate([vb, s_old.astype(BF16)], axis=0), preferred_element_type=F32)
        upd = lax.dot_general(_expand((rt_k[h] * kdec[:, sl]).astype(BF16), gs, seq_mask), vb, TN_DIMS,
                              preferred_element_type=F32)
        rt_ref[:, h] = (g_total[h] * s_old + upd).reshape(gs, DK, DK)
        mu = jnp.mean(o, -1, keepdims=True)
        oc = o - mu
        o = oc * lax.rsqrt(jnp.mean(oc * oc, -1, keepdims=True) + LN_EPS)
        outs_rt.append((o * rtg_ref[:, sl] + rtb_ref[:, sl]) * _silu(rgate[:, sl]))

    outs_hg = []
    for h, sl in enumerate(heads):
        vb = hv[:, sl].astype(BF16)
        s_old = hg_ref[:, h].reshape(gs * DK, DK)
        q_st = _expand(q_in[:, sl].astype(BF16), gs, seq_mask)
        scores = jnp.where(lv == 0, level_sc[h][0], 0.0)
        for j in range(len(levels)):
            scores = jnp.where(lv == j + 1, level_sc[h][j + 1], scores)
        o = jnp.dot(jnp.concatenate([scores.astype(BF16), q_st], axis=1),
                    jnp.concatenate([vb, s_old.astype(BF16)], axis=0), preferred_element_type=F32)
        upd = lax.dot_general(_expand(k_out[:, sl].astype(BF16), gs, seq_mask), vb, TN_DIMS,
                              preferred_element_type=F32)
        hg_ref[:, h] = (_col_bcast(d_rows[:, sl], gs) * s_old + upd).reshape(gs, DK, DK)
        o = o * lax.rsqrt(jnp.mean(o * o, -1, keepdims=True) + LN_EPS)
        outs_hg.append(o * hgn_ref[:, sl] * _silu(hgate[:, sl]))

    return jnp.concatenate(outs_hg + outs_rt, axis=1).astype(BF16)


def _ffn_kernel(cfg, x_ref, wa_ref, wb_ref, cw_ref, cb_ref, wd_ref, l2g_ref, l2b_ref, *rest):
    C, L, Lv = cfg.C, cfg.L, cfg.Lv
    gs = C // L
    if cfg.per_seq:
        buf_ref, y_ref, cs_ref, a_scr = rest
    else:
        init_ref, y_ref, cs_ref, a_scr = rest
    t = pl.program_id(1)

    x = x_ref[...]
    xb = x.astype(BF16)
    if cfg.per_seq:
        pos = lax.broadcasted_iota(jnp.int32, (C, 1), 0) % L
    else:
        @pl.when(t == 0)
        def _():
            a_scr[...] = jnp.zeros((CARRY, D_FF), F32)
            a_scr[CARRY - 2:CARRY, :] = init_ref[0]
        row8 = lax.broadcasted_iota(jnp.int32, (CARRY, 1), 0)

    def up_proj(j):
        cs = slice(j * FF_CHUNK, (j + 1) * FF_CHUNK)
        return (jnp.dot(xb, wa_ref[:, cs], preferred_element_type=F32),
                jnp.dot(xb, wb_ref[:, cs], preferred_element_type=F32))

    n_chunks = D_FF // FF_CHUNK
    ffn = None
    pending = []
    ahead = up_proj(0)
    for j in range(n_chunks):
        cs = slice(j * FF_CHUNK, (j + 1) * FF_CHUNK)
        a, gate = ahead
        if j + 1 < n_chunks:
            ahead = up_proj(j + 1)
        a_m1 = pltpu.roll(a, 1, 0)
        a_m2 = pltpu.roll(a, 2, 0)
        if cfg.per_seq:
            older = jnp.broadcast_to(buf_ref[:, 0:1, cs], (gs, L, FF_CHUNK)).reshape(C, FF_CHUNK)
            newer = jnp.broadcast_to(buf_ref[:, 1:2, cs], (gs, L, FF_CHUNK)).reshape(C, FF_CHUNK)
            a_m1 = jnp.where(pos >= 1, a_m1, newer)
            a_m2 = jnp.where(pos >= 2, a_m2, jnp.where(pos == 1, newer, older))
        else:
            prev = a_scr[:, cs]
            head1 = jnp.where(row8 == 0, prev[CARRY - 1:CARRY, :], a_m1[0:CARRY, :])
            head2 = jnp.where(row8 == 0, prev[CARRY - 2:CARRY - 1, :],
                              jnp.where(row8 == 1, prev[CARRY - 1:CARRY, :], a_m2[0:CARRY, :]))
            a_m1 = jnp.concatenate([head1, a_m1[CARRY:, :]], axis=0)
            a_m2 = jnp.concatenate([head2, a_m2[CARRY:, :]], axis=0)
            a_scr[:, cs] = a[C - CARRY:, :]
        conv = cb_ref[:, cs] + cw_ref[0:1, cs] * a_m2 + cw_ref[1:2, cs] * a_m1 + cw_ref[2:3, cs] * a
        th = jnp.tanh(conv * (GELU_K0 + GELU_K1 * (conv * conv)))
        pending.append(((conv + conv * th) * gate).astype(BF16))
        if len(pending) == DOWN_GROUP or j == n_chunks - 1:
            lo = (j + 1 - len(pending)) * FF_CHUNK
            hidden = pending[0] if len(pending) == 1 else jnp.concatenate(pending, axis=1)
            part = jnp.dot(hidden, wd_ref[lo:(j + 1) * FF_CHUNK, :], preferred_element_type=F32)
            ffn = part if ffn is None else ffn + part
            pending = []
        if cfg.per_seq:
            a_seq = a.reshape(gs, L, FF_CHUNK)
            cs_ref[:, :, cs] = a_seq[:, Lv - 2:Lv, :]
        else:
            cs_ref[0, :, cs] = a[Lv - 2:Lv, :]
    y_ref[...] = _ln(ALPHA * x + ffn, l2g_ref[...], l2b_ref[...])


def _const_spec(shape):
    nd = len(shape)
    return pl.BlockSpec(shape, lambda b, t: (0,) * nd, pipeline_mode=pl.Buffered(1))


def _layer_spec(shape, layer):
    nd = len(shape)
    return pl.BlockSpec((None,) + tuple(shape), lambda b, t: (layer,) + (0,) * nd,
                        pipeline_mode=pl.Buffered(1))


def _params():
    return pltpu.CompilerParams(dimension_semantics=("arbitrary", "arbitrary"),
                                vmem_limit_bytes=VMEM_LIMIT_V7X)


def _mix_call(cfg, x, eg, eb, win, wo, lb, hgn, rtg, rtb, l1g, l1b, cos, sin, hg0, rt0, stacked=None):
    C, L, Lv, nb, nt = cfg.C, cfg.L, cfg.Lv, cfg.nb, cfg.nt
    gs = C // L
    a_stack, lv, dec, qdec, kdec, g_total = _tables(C, L, Lv)
    a_stack = jnp.asarray(a_stack, BF16)
    l = cfg.layer
    step_rows = cfg.sub * C
    rows = nb * nt * step_rows
    assert cfg.sub == 1 or gs == 1
    row_spec = pl.BlockSpec((step_rows, D_MODEL), lambda b, t: (b * nt + t, 0))
    if cfg.per_seq:
        st_spec = pl.BlockSpec((None, gs, HEADS, DK, DK), lambda b, t: (l, b, 0, 0, 0))
    else:
        st_spec = pl.BlockSpec((gs, HEADS, DK, DK), lambda b, t: (0, 0, 0, 0))
    in_specs = [
        row_spec,
        _const_spec((1, D_MODEL)), _const_spec((1, D_MODEL)),
        _layer_spec((D_MODEL, PROJ_COLS), l), _layer_spec((D_MODEL, D_MODEL), l),
        _layer_spec((1, HW), l), _layer_spec((1, HW), l), _layer_spec((1, HW), l), _layer_spec((1, HW), l),
        _layer_spec((1, D_MODEL), l), _layer_spec((1, D_MODEL), l),
        pl.BlockSpec((step_rows, DK), lambda b, t: (t, 0)), pl.BlockSpec((step_rows, DK), lambda b, t: (t, 0)),
        _const_spec(a_stack.shape), _const_spec(lv.shape), _const_spec(dec.shape),
        _const_spec(qdec.shape), _const_spec(kdec.shape),
        st_spec, st_spec,
    ]
    operands = [x, eg, eb, win, wo, lb, hgn, rtg, rtb, l1g, l1b, cos, sin,
                a_stack, jnp.asarray(lv), jnp.asarray(dec), jnp.asarray(qdec), jnp.asarray(kdec), hg0, rt0]
    aliases = {}
    if cfg.per_seq:
        out_st = pl.BlockSpec((None, gs, HEADS, DK, DK), lambda b, t: (l, b, 0, 0, 0))
        st_shape = jax.ShapeDtypeStruct((DEPTH, nb * gs, HEADS, DK, DK), F32)
        if stacked is not None:
            aliases = {len(operands): 1, len(operands) + 1: 2}
            in_specs += [pl.BlockSpec(memory_space=pl.ANY), pl.BlockSpec(memory_space=pl.ANY)]
            operands += list(stacked)
    else:
        out_st = pl.BlockSpec((gs, HEADS, DK, DK), lambda b, t: (b, 0, 0, 0))
        st_shape = jax.ShapeDtypeStruct((nb * gs, HEADS, DK, DK), F32)
    return pl.pallas_call(
        functools.partial(_mix_kernel, cfg, g_total),
        out_shape=(jax.ShapeDtypeStruct((rows, D_MODEL), F32), st_shape, st_shape),
        grid=(nb, nt),
        in_specs=in_specs,
        out_specs=(row_spec, out_st, out_st),
        input_output_aliases=aliases,
        compiler_params=_params(),
        name=f"mixer_l{l}_L{L}_nt{nt}",
    )(*operands)


def _ffn_call(cfg, x, wa, wb, cw, cb, wd, l2g, l2b, *conv_in):
    C, L, nb, nt = cfg.C, cfg.L, cfg.nb, cfg.nt
    gs = C // L
    l = cfg.layer
    rows = nb * nt * C
    row_spec = pl.BlockSpec((C, D_MODEL), lambda b, t: (b * nt + t, 0))
    in_specs = [
        row_spec,
        _layer_spec((D_MODEL, D_FF), l), _layer_spec((D_MODEL, D_FF), l),
        _layer_spec((CONV_W, D_FF), l), _layer_spec((1, D_FF), l),
        _layer_spec((D_FF, D_MODEL), l),
        _layer_spec((1, D_MODEL), l), _layer_spec((1, D_MODEL), l),
    ]
    if cfg.per_seq:
        in_specs += [pl.BlockSpec((None, gs, CONV_W - 1, D_FF), lambda b, t: (l, b, 0, 0))]
        cs_shape = jax.ShapeDtypeStruct((nb * gs, CONV_W - 1, D_FF), F32)
        cs_spec = pl.BlockSpec((gs, CONV_W - 1, D_FF), lambda b, t: (b, 0, 0))
    else:
        in_specs += [pl.BlockSpec((1, CONV_W - 1, D_FF), lambda b, t: (0, 0, 0))]
        cs_shape = jax.ShapeDtypeStruct((nb, CONV_W - 1, D_FF), F32)
        cs_spec = pl.BlockSpec((1, CONV_W - 1, D_FF), lambda b, t: (b, 0, 0))
    return pl.pallas_call(
        functools.partial(_ffn_kernel, cfg),
        out_shape=(jax.ShapeDtypeStruct((rows, D_MODEL), F32), cs_shape),
        grid=(nb, nt),
        in_specs=in_specs,
        out_specs=(row_spec, cs_spec),
        scratch_shapes=[pltpu.VMEM((CARRY, D_FF), F32)],
        compiler_params=_params(),
        name=f"convffn_l{l}_L{L}_nt{nt}",
    )(x, wa, wb, cw, cb, wd, l2g, l2b, *conv_in)


def _rope_tables(pos):
    half = DK // 2
    inv = ROPE_BASE ** (-jnp.arange(half, dtype=F32) / half)
    ang = pos.astype(F32)[:, None] * inv[None, :]
    cos, sin = jnp.cos(ang), jnp.sin(ang)
    cos2 = jnp.concatenate([cos, cos], axis=1)
    sin2 = jnp.concatenate([-sin, sin], axis=1)
    return cos2, sin2


def kernel(x_prompt, x_sample, state_hgrn, state_ret, state_conv, meta_tokens, emb_ln_g, emb_ln_b, lb_logits, w_in, hg_norm_g, rt_gn_g, rt_gn_b, w_o, ln1_g, ln1_b, w_a, w_b, conv_w, conv_b, w_down, ln2_g, ln2_b):
    bsz, seq, _ = x_prompt.shape
    dec_b, dec_t, _ = x_sample.shape
    C = TILE
    assert seq % (C * MIX_SUB) == 0 and seq % FFN_TILE == 0 and N_META <= C and dec_t >= CONV_W - 1
    L_s = 8
    assert dec_t <= L_s and (dec_b * L_s) % C == 0

    sm = jax.nn.softmax(lb_logits.astype(F32), axis=0)
    lb_all = (jnp.cumsum(sm, axis=0) - sm[0:1])[:, None, :]
    win, wo = w_in.astype(BF16), w_o.astype(BF16)
    wa, wb, wd = w_a.astype(BF16), (0.5 * w_b).astype(BF16), w_down.astype(BF16)
    vec = lambda v: v[:, None, :]
    hgn, rtg, rtb = vec(hg_norm_g), vec(rt_gn_g), vec(rt_gn_b)
    l1g, l1b, l2g, l2b, cb = vec(ln1_g), vec(ln1_b), vec(ln2_g), vec(ln2_b), vec(conv_b)
    eg, eb = emb_ln_g[None, :], emb_ln_b[None, :]

    cos_m, sin_m = _rope_tables(jnp.arange(C, dtype=jnp.int32))
    cos_p, sin_p = _rope_tables(N_META + jnp.arange(seq, dtype=jnp.int32))
    cos_s, sin_s = _rope_tables(PAST_LEN + jnp.arange(C, dtype=jnp.int32) % L_s)

    xm = jnp.pad(meta_tokens.astype(F32), ((0, C - N_META), (0, 0)))
    xp = x_prompt.reshape(bsz * seq, D_MODEL)
    xs = jnp.pad(x_sample, ((0, 0), (0, L_s - dec_t), (0, 0))).reshape(dec_b * L_s, D_MODEL)
    zero_state = jnp.zeros((1, HEADS, DK, DK), F32)
    zero_conv = jnp.zeros((1, CONV_W - 1, D_FF), F32)

    hg_p, rt_p, cv_p, cv_s = [], [], [], []
    st_s = (jnp.zeros_like(state_hgrn, dtype=F32), jnp.zeros_like(state_ret, dtype=F32))
    for l in range(DEPTH):
        emb = l == 0
        cfg_m = Cfg(1, 1, C, C, N_META, emb, False, l)
        cfg_p = Cfg(bsz, seq // (C * MIX_SUB), C, C, C, emb, False, l, MIX_SUB)
        cfg_s = Cfg(dec_b * L_s // C, 1, C, L_s, dec_t, emb, True, l)
        mix_w = (eg, eb, win, wo, lb_all, hgn, rtg, rtb, l1g, l1b)
        ffn_w = (wa, wb, conv_w, cb, wd, l2g, l2b)

        xm1, hg_m, rt_m = _mix_call(cfg_m, xm, *mix_w, cos_m, sin_m, zero_state, zero_state)
        xm, cv_m = _ffn_call(cfg_m, xm1, *ffn_w, zero_conv)

        xp1, hg, rt = _mix_call(cfg_p, xp, *mix_w, cos_p, sin_p, hg_m, rt_m)
        cfg_pf = Cfg(bsz, seq // FFN_TILE, FFN_TILE, FFN_TILE, FFN_TILE, False, False, l)
        xp, cv = _ffn_call(cfg_pf, xp1, *ffn_w, cv_m)
        hg_p.append(hg)
        rt_p.append(rt)
        cv_p.append(cv)

        xs1, *st_s = _mix_call(cfg_s, xs, *mix_w, cos_s, sin_s, state_hgrn, state_ret, stacked=st_s)
        cfg_sf = Cfg(dec_b * L_s // FFN_TILE_SAMPLE, 1, FFN_TILE_SAMPLE, L_s, dec_t, False, True, l)
        xs, cv = _ffn_call(cfg_sf, xs1, *ffn_w, state_conv)
        cv_s.append(cv)

    y_prompt = xp.reshape(bsz, seq, D_MODEL)
    y_sample = xs.reshape(dec_b, L_s, D_MODEL)[:, :dec_t]
    hg_s, rt_s = st_s
    return (y_prompt, y_sample, jnp.stack(hg_p), hg_s, jnp.stack(rt_p), rt_s, jnp.stack(cv_p), jnp.stack(cv_s))
```

```python
import functools
from typing import NamedTuple

import numpy as np
import jax
import jax.numpy as jnp
from jax import lax
from jax.experimental import pallas as pl
from jax.experimental.pallas import tpu as pltpu

D_MODEL = 1024
DEPTH = 4
HEADS = 4
DK = 128
HW = HEADS * DK
PROJ_COLS = 8 * HW
N_META = 16
PAST_LEN = 16384
CONV_W = 3
D_FF = 2816
ROPE_BASE = 10000.0
LN_EPS = 1e-5
F_EPS = 1e-6
ALPHA = (2 * DEPTH) ** 0.25

TILE = 128
MIX_SUB = 4
N_PARTS = PROJ_COLS // HW
FFN_TILE = 1024
FFN_TILE_SAMPLE = 256
FF_CHUNK = 256
GELU_K0 = float((2.0 / np.pi) ** 0.5)
GELU_K1 = 0.044715 * GELU_K0
CARRY = 8
VMEM_LIMIT_V7X = 56 * 1024 * 1024

F32 = jnp.float32
BF16 = jnp.bfloat16
NT_DIMS = (((1,), (1,)), ((), ()))
TN_DIMS = (((0,), (0,)), ((), ()))


class Cfg(NamedTuple):
    nb: int
    nt: int
    C: int
    L: int
    Lv: int
    embed: bool
    per_seq: bool
    layer: int
    sub: int = 1


def _levels(Lv):
    return [m for m in (1, 2, 4, 8, 16, 32, 64, 128, 256) if m < Lv]


@functools.lru_cache(maxsize=None)
def _tables(C, L, Lv):
    r = np.arange(C)
    g, p = r // L, r % L
    same = g[:, None] == g[None, :]
    pt, pu = p[:, None], p[None, :]
    blocks = [same & (pu <= pt), same & (pu > pt)]
    lv = np.full((C, C), -1, np.int32)
    ok = same & (pt < Lv) & (pu < Lv)
    lv[ok & (pt == pu)] = 0
    for j, m in enumerate(_levels(Lv)):
        blk_t, blk_u = pt // (2 * m), pu // (2 * m)
        half_t, half_u = (pt // m) % 2, (pu // m) % 2
        lv[ok & same & (blk_t == blk_u) & (half_t == 1) & (half_u == 0)] = j + 1
    a_stack = np.concatenate(blocks, 0).astype(np.float32)
    a_stack = np.concatenate([a_stack, a_stack], 1)

    log_gamma = np.log1p(-np.exp2(-5.0 - np.arange(HEADS, dtype=np.float64)))
    rel = (pt - pu).astype(np.float64)
    causal = same & (pu <= pt) & (pu < Lv)
    dec = np.where(causal[None], np.exp(np.where(causal, rel, 0.0)[None] * log_gamma[:, None, None]), 0.0)
    qdec = np.exp((p + 1.0)[:, None] * log_gamma[None, :])
    kdec = np.where((p < Lv)[:, None], np.exp((Lv - 1.0 - p)[:, None] * log_gamma[None, :]), 0.0)
    qdec = np.repeat(qdec, DK, axis=1)
    kdec = np.repeat(kdec, DK, axis=1)
    g_total = tuple(float(x) for x in np.exp(Lv * log_gamma))
    return (a_stack, lv, dec.astype(np.float32), qdec.astype(np.float32), kdec.astype(np.float32), g_total)


def _ln(x, g, b):
    mu = jnp.mean(x, -1, keepdims=True)
    xc = x - mu
    var = jnp.mean(xc * xc, -1, keepdims=True)
    return xc * lax.rsqrt(var + LN_EPS) * g + b


def _silu(x):
    h = 0.5 * x
    return h + h * jnp.tanh(h)


def _expand(x, gs, mask):
    if gs == 1:
        return x
    return jnp.where(mask, jnp.concatenate([x] * gs, axis=1), jnp.zeros((), x.dtype))


def _col_bcast(rows, gs):
    outs = []
    for g in range(gs):
        outs.append(jnp.broadcast_to(rows[g:g + 1, :], (DK, DK)).T)
    return outs[0] if gs == 1 else jnp.concatenate(outs, axis=0)


def _boundary_rows(b, m):
    C, W = b.shape
    blk = 2 * m
    if blk >= 8:
        b3 = b.reshape(C // blk, blk, W)
        return jnp.broadcast_to(b3[:, m - 1:m, :], b3.shape).reshape(C, W)
    b3 = b.reshape(C // 8, 8, W)
    sub = lax.broadcasted_iota(jnp.int32, (1, 8, 1), 1)
    r = jnp.broadcast_to(b3[:, m - 1:m, :], b3.shape)
    for k in range(1, 8 // blk):
        r = jnp.where(sub >= k * blk, jnp.broadcast_to(b3[:, k * blk + m - 1:k * blk + m, :], b3.shape), r)
    return r.reshape(C, W)


def _mix_kernel(cfg, g_total, x_ref, eg_ref, eb_ref, win_ref, wo_ref, lb_ref, hgn_ref, rtg_ref, rtb_ref,
                l1g_ref, l1b_ref, cos_ref, sin_ref, a_ref, lv_ref, dec_ref, qdec_ref, kdec_ref,
                hg0_ref, rt0_ref, *rest):
    x1_ref, hg_ref, rt_ref = rest[-3:]
    C = cfg.C
    t = pl.program_id(1)

    @pl.when(t == 0)
    def _():
        hg_ref[...] = hg0_ref[...]
        rt_ref[...] = rt0_ref[...]

    x = x_ref[...]
    if cfg.embed:
        x = _ln(x, eg_ref[...], eb_ref[...])
    xb = x.astype(BF16)
    parts = [jnp.dot(xb, win_ref[:, k * HW:(k + 1) * HW], preferred_element_type=F32) for k in range(N_PARTS)]
    outs = []
    for s in range(cfg.sub):
        rows = slice(s * C, (s + 1) * C)
        outs.append(_mix_tile(cfg, g_total, [p[rows, :] for p in parts], cos_ref[rows, :], sin_ref[rows, :],
                              lb_ref, hgn_ref, rtg_ref, rtb_ref, a_ref, lv_ref, dec_ref, qdec_ref, kdec_ref,
                              hg_ref, rt_ref))
    o_all = outs[0] if cfg.sub == 1 else jnp.concatenate(outs, axis=0)
    mix = jnp.dot(o_all, wo_ref[...], preferred_element_type=F32)
    x1_ref[...] = _ln(ALPHA * x + mix, l1g_ref[...], l1b_ref[...])


def _mix_tile(cfg, g_total, parts, cos, sin, lb_ref, hgn_ref, rtg_ref, rtb_ref, a_ref, lv_ref, dec_ref,
              qdec_ref, kdec_ref, hg_ref, rt_ref):
    C, L, Lv = cfg.C, cfg.L, cfg.Lv
    gs = C // L
    levels = _levels(Lv)
    hq, hf, hv, hgate, rq, rk, rv, rgate = parts
    heads = [slice(h * DK, (h + 1) * DK) for h in range(HEADS)]

    if gs > 1:
        row_seq = lax.broadcasted_iota(jnp.int32, (C, gs * DK), 0) // L
        lane_blk = lax.broadcasted_iota(jnp.int32, (C, gs * DK), 1) // DK
        seq_mask = row_seq == lane_blk
    else:
        seq_mask = None
    lv = lv_ref[...]


    qdec, kdec = qdec_ref[...], kdec_ref[...]
    rt_q, rt_k, rt_sc = [], [], []
    for h, sl in enumerate(heads):
        qh = rq[:, sl] * cos + pltpu.roll(rq[:, sl], DK // 2, 1) * sin
        kh = (rk[:, sl] * cos + pltpu.roll(rk[:, sl], DK // 2, 1) * sin) * (DK ** -0.5)
        sc = lax.dot_general(qh.astype(BF16), kh.astype(BF16), NT_DIMS, preferred_element_type=F32)
        rt_q.append(qh)
        rt_k.append(kh)
        rt_sc.append((sc * dec_ref[h]).astype(BF16))

    lb = lb_ref[...]
    th = jnp.tanh(0.5 * hf)
    f_gate = lb + (1.0 - lb) * (0.5 + 0.5 * th)
    lf = jnp.log(jnp.maximum(f_gate, F_EPS))
    kx = (1.0 - lb) * (0.5 - 0.5 * th)
    q = _silu(hq)
    if Lv < L:
        real = (lax.broadcasted_iota(jnp.int32, (C, 1), 0) % L) < Lv
        lf = jnp.where(real, lf, 0.0)
        kx = jnp.where(real, kx, 0.0)
    lf_hi = lf.astype(BF16)
    rem = lf - lf_hi.astype(F32)
    lf_mid = rem.astype(BF16)
    lf_lo = (rem - lf_mid.astype(F32)).astype(BF16)
    sums = (jnp.dot(a_ref[...], jnp.concatenate([lf_hi, lf_mid], axis=0), preferred_element_type=F32)
            + jnp.dot(a_ref[:, 0:C], lf_lo, preferred_element_type=F32))
    b_incl = sums[0:C]
    q_in = q * jnp.exp(b_incl)
    k_out = kx * jnp.exp(sums[C:2 * C])
    if gs == 1:
        b_last = b_incl[Lv - 1:Lv, :]
    else:
        b_last = b_incl.reshape(gs, L, HW)[:, Lv - 1, :]
    d_rows = jnp.exp(b_last)

    qb, kb = q.astype(BF16), kx.astype(BF16)
    level_sc = [[lax.dot_general(qb[:, sl], kb[:, sl], NT_DIMS, preferred_element_type=F32)] for sl in heads]
    for m in levels:
        e = jnp.exp(-jnp.abs(b_incl - _boundary_rows(b_incl, m))).astype(BF16)
        qe, ke = qb * e, kb * e
        for h, sl in enumerate(heads):
            level_sc[h].append(lax.dot_general(qe[:, sl], ke[:, sl], NT_DIMS, preferred_element_type=F32))

    outs_rt = []
    for h, sl in enumerate(heads):
        vb = rv[:, sl].astype(BF16)
        s_old = rt_ref[:, h].reshape(gs * DK, DK)
        q_st = _expand((rt_q[h] * qdec[:, sl]).astype(BF16), gs, seq_mask)
        o = jnp.dot(jnp.concatenate([rt_sc[h], q_st], axis=1),
                    jnp.concatenate([vb, s_old.astype(BF16)], axis=0), preferred_element_type=F32)
        upd = lax.dot_general(_expand((rt_k[h] * kdec[:, sl]).astype(BF16), gs, seq_mask), vb, TN_DIMS,
                              preferred_element_type=F32)
        rt_ref[:, h] = (g_total[h] * s_old + upd).reshape(gs, DK, DK)
        mu = jnp.mean(o, -1, keepdims=True)
        oc = o - mu
        o = oc * lax.rsqrt(jnp.mean(oc * oc, -1, keepdims=True) + LN_EPS)
        outs_rt.append((o * rtg_ref[:, sl] + rtb_ref[:, sl]) * _silu(rgate[:, sl]))

    outs_hg = []
    for h, sl in enumerate(heads):
        vb = hv[:, sl].astype(BF16)
        s_old = hg_ref[:, h].reshape(gs * DK, DK)
        q_st = _expand(q_in[:, sl].astype(BF16), gs, seq_mask)
        scores = jnp.where(lv == 0, level_sc[h][0], 0.0)
        for j in range(len(levels)):
            scores = jnp.where(lv == j + 1, level_sc[h][j + 1], scores)
        o = jnp.dot(jnp.concatenate([scores.astype(BF16), q_st], axis=1),
                    jnp.concatenate([vb, s_old.astype(BF16)], axis=0), preferred_element_type=F32)
        upd = lax.dot_general(_expand(k_out[:, sl].astype(BF16), gs, seq_mask), vb, TN_DIMS,
                              preferred_element_type=F32)
        hg_ref[:, h] = (_col_bcast(d_rows[:, sl], gs) * s_old + upd).reshape(gs, DK, DK)
        o = o * lax.rsqrt(jnp.mean(o * o, -1, keepdims=True) + LN_EPS)
        outs_hg.append(o * hgn_ref[:, sl] * _silu(hgate[:, sl]))

    return jnp.concatenate(outs_hg + outs_rt, axis=1).astype(BF16)


def _ffn_kernel(cfg, x_ref, wa_ref, wb_ref, cw_ref, cb_ref, wd_ref, l2g_ref, l2b_ref, *rest):
    C, L, Lv = cfg.C, cfg.L, cfg.Lv
    gs = C // L
    if cfg.per_seq:
        buf_ref, y_ref, cs_ref, a_scr = rest
    else:
        init_ref, y_ref, cs_ref, a_scr = rest
    t = pl.program_id(1)

    x = x_ref[...]
    xb = x.astype(BF16)
    if cfg.per_seq:
        pos = lax.broadcasted_iota(jnp.int32, (C, 1), 0) % L
    else:
        @pl.when(t == 0)
        def _():
            a_scr[...] = jnp.zeros((CARRY, D_FF), F32)
            a_scr[CARRY - 2:CARRY, :] = init_ref[0]
        row8 = lax.broadcasted_iota(jnp.int32, (CARRY, 1), 0)

    def up_proj(j):
        cs = slice(j * FF_CHUNK, (j + 1) * FF_CHUNK)
        return (jnp.dot(xb, wa_ref[:, cs], preferred_element_type=F32),
                jnp.dot(xb, wb_ref[:, cs], preferred_element_type=F32))

    n_chunks = D_FF // FF_CHUNK
    hidden = []
    ahead = up_proj(0)
    for j in range(n_chunks):
        cs = slice(j * FF_CHUNK, (j + 1) * FF_CHUNK)
        a, gate = ahead
        if j + 1 < n_chunks:
            ahead = up_proj(j + 1)
        a_m1 = pltpu.roll(a, 1, 0)
        a_m2 = pltpu.roll(a, 2, 0)
        if cfg.per_seq:
            older = jnp.broadcast_to(buf_ref[:, 0:1, cs], (gs, L, FF_CHUNK)).reshape(C, FF_CHUNK)
            newer = jnp.broadcast_to(buf_ref[:, 1:2, cs], (gs, L, FF_CHUNK)).reshape(C, FF_CHUNK)
            a_m1 = jnp.where(pos >= 1, a_m1, newer)
            a_m2 = jnp.where(pos >= 2, a_m2, jnp.where(pos == 1, newer, older))
        else:
            prev = a_scr[:, cs]
            head1 = jnp.where(row8 == 0, prev[CARRY - 1:CARRY, :], a_m1[0:CARRY, :])
            head2 = jnp.where(row8 == 0, prev[CARRY - 2:CARRY - 1, :],
                              jnp.where(row8 == 1, prev[CARRY - 1:CARRY, :], a_m2[0:CARRY, :]))
            a_m1 = jnp.concatenate([head1, a_m1[CARRY:, :]], axis=0)
            a_m2 = jnp.concatenate([head2, a_m2[CARRY:, :]], axis=0)
            a_scr[:, cs] = a[C - CARRY:, :]
        conv = cb_ref[:, cs] + cw_ref[0:1, cs] * a_m2 + cw_ref[1:2, cs] * a_m1 + cw_ref[2:3, cs] * a
        th = jnp.tanh(conv * (GELU_K0 + GELU_K1 * (conv * conv)))
        hidden.append(((conv + conv * th) * gate).astype(BF16))
        if cfg.per_seq:
            a_seq = a.reshape(gs, L, FF_CHUNK)
            cs_ref[:, :, cs] = a_seq[:, Lv - 2:Lv, :]
        else:
            cs_ref[0, :, cs] = a[Lv - 2:Lv, :]
    ffn = jnp.dot(jnp.concatenate(hidden, axis=1), wd_ref[...], preferred_element_type=F32)
    y_ref[...] = _ln(ALPHA * x + ffn, l2g_ref[...], l2b_ref[...])


def _const_spec(shape):
    nd = len(shape)
    return pl.BlockSpec(shape, lambda b, t: (0,) * nd, pipeline_mode=pl.Buffered(1))


def _layer_spec(shape, layer):
    nd = len(shape)
    return pl.BlockSpec((None,) + tuple(shape), lambda b, t: (layer,) + (0,) * nd,
                        pipeline_mode=pl.Buffered(1))


def _params():
    return pltpu.CompilerParams(dimension_semantics=("arbitrary", "arbitrary"),
                                vmem_limit_bytes=VMEM_LIMIT_V7X)


def _mix_call(cfg, x, eg, eb, win, wo, lb, hgn, rtg, rtb, l1g, l1b, cos, sin, hg0, rt0, stacked=None):
    C, L, Lv, nb, nt = cfg.C, cfg.L, cfg.Lv, cfg.nb, cfg.nt
    gs = C // L
    a_stack, lv, dec, qdec, kdec, g_total = _tables(C, L, Lv)
    a_stack = jnp.asarray(a_stack, BF16)
    l = cfg.layer
    step_rows = cfg.sub * C
    rows = nb * nt * step_rows
    assert cfg.sub == 1 or gs == 1
    row_spec = pl.BlockSpec((step_rows, D_MODEL), lambda b, t: (b * nt + t, 0))
    if cfg.per_seq:
        st_spec = pl.BlockSpec((None, gs, HEADS, DK, DK), lambda b, t: (l, b, 0, 0, 0))
    else:
        st_spec = pl.BlockSpec((gs, HEADS, DK, DK), lambda b, t: (0, 0, 0, 0))
    in_specs = [
        row_spec,
        _const_spec((1, D_MODEL)), _const_spec((1, D_MODEL)),
        _layer_spec((D_MODEL, PROJ_COLS), l), _layer_spec((D_MODEL, D_MODEL), l),
        _layer_spec((1, HW), l), _layer_spec((1, HW), l), _layer_spec((1, HW), l), _layer_spec((1, HW), l),
        _layer_spec((1, D_MODEL), l), _layer_spec((1, D_MODEL), l),
        pl.BlockSpec((step_rows, DK), lambda b, t: (t, 0)), pl.BlockSpec((step_rows, DK), lambda b, t: (t, 0)),
        _const_spec(a_stack.shape), _const_spec(lv.shape), _const_spec(dec.shape),
        _const_spec(qdec.shape), _const_spec(kdec.shape),
        st_spec, st_spec,
    ]
    operands = [x, eg, eb, win, wo, lb, hgn, rtg, rtb, l1g, l1b, cos, sin,
                a_stack, jnp.asarray(lv), jnp.asarray(dec), jnp.asarray(qdec), jnp.asarray(kdec), hg0, rt0]
    aliases = {}
    if cfg.per_seq:
        out_st = pl.BlockSpec((None, gs, HEADS, DK, DK), lambda b, t: (l, b, 0, 0, 0))
        st_shape = jax.ShapeDtypeStruct((DEPTH, nb * gs, HEADS, DK, DK), F32)
        if stacked is not None:
            aliases = {len(operands): 1, len(operands) + 1: 2}
            in_specs += [pl.BlockSpec(memory_space=pl.ANY), pl.BlockSpec(memory_space=pl.ANY)]
            operands += list(stacked)
    else:
        out_st = pl.BlockSpec((gs, HEADS, DK, DK), lambda b, t: (b, 0, 0, 0))
        st_shape = jax.ShapeDtypeStruct((nb * gs, HEADS, DK, DK), F32)
    return pl.pallas_call(
        functools.partial(_mix_kernel, cfg, g_total),
        out_shape=(jax.ShapeDtypeStruct((rows, D_MODEL), F32), st_shape, st_shape),
        grid=(nb, nt),
        in_specs=in_specs,
        out_specs=(row_spec, out_st, out_st),
        input_output_aliases=aliases,
        compiler_params=_params(),
        name=f"mixer_l{l}_L{L}_nt{nt}",
    )(*operands)


def _ffn_call(cfg, x, wa, wb, cw, cb, wd, l2g, l2b, *conv_in):
    C, L, nb, nt = cfg.C, cfg.L, cfg.nb, cfg.nt
    gs = C // L
    l = cfg.layer
    rows = nb * nt * C
    row_spec = pl.BlockSpec((C, D_MODEL), lambda b, t: (b * nt + t, 0))
    in_specs = [
        row_spec,
        _layer_spec((D_MODEL, D_FF), l), _layer_spec((D_MODEL, D_FF), l),
        _layer_spec((CONV_W, D_FF), l), _layer_spec((1, D_FF), l),
        _layer_spec((D_FF, D_MODEL), l),
        _layer_spec((1, D_MODEL), l), _layer_spec((1, D_MODEL), l),
    ]
    if cfg.per_seq:
        in_specs += [pl.BlockSpec((None, gs, CONV_W - 1, D_FF), lambda b, t: (l, b, 0, 0))]
        cs_shape = jax.ShapeDtypeStruct((nb * gs, CONV_W - 1, D_FF), F32)
        cs_spec = pl.BlockSpec((gs, CONV_W - 1, D_FF), lambda b, t: (b, 0, 0))
    else:
        in_specs += [pl.BlockSpec((1, CONV_W - 1, D_FF), lambda b, t: (0, 0, 0))]
        cs_shape = jax.ShapeDtypeStruct((nb, CONV_W - 1, D_FF), F32)
        cs_spec = pl.BlockSpec((1, CONV_W - 1, D_FF), lambda b, t: (b, 0, 0))
    return pl.pallas_call(
        functools.partial(_ffn_kernel, cfg),
        out_shape=(jax.ShapeDtypeStruct((rows, D_MODEL), F32), cs_shape),
        grid=(nb, nt),
        in_specs=in_specs,
        out_specs=(row_spec, cs_spec),
        scratch_shapes=[pltpu.VMEM((CARRY, D_FF), F32)],
        compiler_params=_params(),
        name=f"convffn_l{l}_L{L}_nt{nt}",
    )(x, wa, wb, cw, cb, wd, l2g, l2b, *conv_in)


def _rope_tables(pos):
    half = DK // 2
    inv = ROPE_BASE ** (-jnp.arange(half, dtype=F32) / half)
    ang = pos.astype(F32)[:, None] * inv[None, :]
    cos, sin = jnp.cos(ang), jnp.sin(ang)
    cos2 = jnp.concatenate([cos, cos], axis=1)
    sin2 = jnp.concatenate([-sin, sin], axis=1)
    return cos2, sin2


def kernel(x_prompt, x_sample, state_hgrn, state_ret, state_conv, meta_tokens, emb_ln_g, emb_ln_b, lb_logits, w_in, hg_norm_g, rt_gn_g, rt_gn_b, w_o, ln1_g, ln1_b, w_a, w_b, conv_w, conv_b, w_down, ln2_g, ln2_b):
    bsz, seq, _ = x_prompt.shape
    dec_b, dec_t, _ = x_sample.shape
    C = TILE
    assert seq % (C * MIX_SUB) == 0 and seq % FFN_TILE == 0 and N_META <= C and dec_t >= CONV_W - 1
    L_s = 8
    assert dec_t <= L_s and (dec_b * L_s) % C == 0

    sm = jax.nn.softmax(lb_logits.astype(F32), axis=0)
    lb_all = (jnp.cumsum(sm, axis=0) - sm[0:1])[:, None, :]
    win, wo = w_in.astype(BF16), w_o.astype(BF16)
    wa, wb, wd = w_a.astype(BF16), (0.5 * w_b).astype(BF16), w_down.astype(BF16)
    vec = lambda v: v[:, None, :]
    hgn, rtg, rtb = vec(hg_norm_g), vec(rt_gn_g), vec(rt_gn_b)
    l1g, l1b, l2g, l2b, cb = vec(ln1_g), vec(ln1_b), vec(ln2_g), vec(ln2_b), vec(conv_b)
    eg, eb = emb_ln_g[None, :], emb_ln_b[None, :]

    cos_m, sin_m = _rope_tables(jnp.arange(C, dtype=jnp.int32))
    cos_p, sin_p = _rope_tables(N_META + jnp.arange(seq, dtype=jnp.int32))
    cos_s, sin_s = _rope_tables(PAST_LEN + jnp.arange(C, dtype=jnp.int32) % L_s)

    xm = jnp.pad(meta_tokens.astype(F32), ((0, C - N_META), (0, 0)))
    xp = x_prompt.reshape(bsz * seq, D_MODEL)
    xs = jnp.pad(x_sample, ((0, 0), (0, L_s - dec_t), (0, 0))).reshape(dec_b * L_s, D_MODEL)
    zero_state = jnp.zeros((1, HEADS, DK, DK), F32)
    zero_conv = jnp.zeros((1, CONV_W - 1, D_FF), F32)

    hg_p, rt_p, cv_p, cv_s = [], [], [], []
    st_s = (jnp.zeros_like(state_hgrn, dtype=F32), jnp.zeros_like(state_ret, dtype=F32))
    for l in range(DEPTH):
        emb = l == 0
        cfg_m = Cfg(1, 1, C, C, N_META, emb, False, l)
        cfg_p = Cfg(bsz, seq // (C * MIX_SUB), C, C, C, emb, False, l, MIX_SUB)
        cfg_s = Cfg(dec_b * L_s // C, 1, C, L_s, dec_t, emb, True, l)
        mix_w = (eg, eb, win, wo, lb_all, hgn, rtg, rtb, l1g, l1b)
        ffn_w = (wa, wb, conv_w, cb, wd, l2g, l2b)

        xm1, hg_m, rt_m = _mix_call(cfg_m, xm, *mix_w, cos_m, sin_m, zero_state, zero_state)
        xm, cv_m = _ffn_call(cfg_m, xm1, *ffn_w, zero_conv)

        xp1, hg, rt = _mix_call(cfg_p, xp, *mix_w, cos_p, sin_p, hg_m, rt_m)
        cfg_pf = Cfg(bsz, seq // FFN_TILE, FFN_TILE, FFN_TILE, FFN_TILE, False, False, l)
        xp, cv = _ffn_call(cfg_pf, xp1, *ffn_w, cv_m)
        hg_p.append(hg)
        rt_p.append(rt)
        cv_p.append(cv)

        xs1, *st_s = _mix_call(cfg_s, xs, *mix_w, cos_s, sin_s, state_hgrn, state_ret, stacked=st_s)
        cfg_sf = Cfg(dec_b * L_s // FFN_TILE_SAMPLE, 1, FFN_TILE_SAMPLE, L_s, dec_t, False, True, l)
        xs, cv = _ffn_call(cfg_sf, xs1, *ffn_w, state_conv)
        cv_s.append(cv)

    y_prompt = xp.reshape(bsz, seq, D_MODEL)
    y_sample = xs.reshape(dec_b, L_s, D_MODEL)[:, :dec_t]
    hg_s, rt_s = st_s
    return (y_prompt, y_sample, jnp.stack(hg_p), hg_s, jnp.stack(rt_p), rt_s, jnp.stack(cv_p), jnp.stack(cv_s))
```

```python
import functools
from typing import NamedTuple

import numpy as np
import jax
import jax.numpy as jnp
from jax import lax
from jax.experimental import pallas as pl
from jax.experimental.pallas import tpu as pltpu

D_MODEL = 1024
DEPTH = 4
HEADS = 4
DK = 128
HW = HEADS * DK
PROJ_COLS = 8 * HW
N_META = 16
PAST_LEN = 16384
CONV_W = 3
D_FF = 2816
ROPE_BASE = 10000.0
LN_EPS = 1e-5
F_EPS = 1e-6
ALPHA = (2 * DEPTH) ** 0.25

TILE = 128
MIX_SUB = 4
N_PARTS = PROJ_COLS // HW
FFN_TILE = 1024
FFN_TILE_SAMPLE = 256
FF_CHUNK = 256
GELU_K0 = float((2.0 / np.pi) ** 0.5)
GELU_K1 = 0.044715 * GELU_K0
CARRY = 8
VMEM_LIMIT_V7X = 56 * 1024 * 1024

F32 = jnp.float32
BF16 = jnp.bfloat16
NT_DIMS = (((1,), (1,)), ((), ()))
TN_DIMS = (((0,), (0,)), ((), ()))


class Cfg(NamedTuple):
    nb: int
    nt: int
    C: int
    L: int
    Lv: int
    embed: bool
    per_seq: bool
    layer: int
    sub: int = 1


def _levels(Lv):
    return [m for m in (1, 2, 4, 8, 16, 32, 64, 128, 256) if m < Lv]


@functools.lru_cache(maxsize=None)
def _tables(C, L, Lv):
    r = np.arange(C)
    g, p = r // L, r % L
    same = g[:, None] == g[None, :]
    pt, pu = p[:, None], p[None, :]
    blocks = [same & (pu <= pt), same & (pu > pt)]
    lv = np.full((C, C), -1, np.int32)
    ok = same & (pt < Lv) & (pu < Lv)
    lv[ok & (pt == pu)] = 0
    for j, m in enumerate(_levels(Lv)):
        blk_t, blk_u = pt // (2 * m), pu // (2 * m)
        half_t, half_u = (pt // m) % 2, (pu // m) % 2
        lv[ok & same & (blk_t == blk_u) & (half_t == 1) & (half_u == 0)] = j + 1
    a_stack = np.concatenate(blocks, 0).astype(np.float32)
    a_stack = np.concatenate([a_stack, a_stack], 1)

    log_gamma = np.log1p(-np.exp2(-5.0 - np.arange(HEADS, dtype=np.float64)))
    rel = (pt - pu).astype(np.float64)
    causal = same & (pu <= pt) & (pu < Lv)
    dec = np.where(causal[None], np.exp(np.where(causal, rel, 0.0)[None] * log_gamma[:, None, None]), 0.0)
    qdec = np.exp((p + 1.0)[:, None] * log_gamma[None, :])
    kdec = np.where((p < Lv)[:, None], np.exp((Lv - 1.0 - p)[:, None] * log_gamma[None, :]), 0.0)
    qdec = np.repeat(qdec, DK, axis=1)
    kdec = np.repeat(kdec, DK, axis=1)
    g_total = tuple(float(x) for x in np.exp(Lv * log_gamma))
    return (a_stack, lv, dec.astype(np.float32), qdec.astype(np.float32), kdec.astype(np.float32), g_total)


def _ln(x, g, b):
    mu = jnp.mean(x, -1, keepdims=True)
    xc = x - mu
    var = jnp.mean(xc * xc, -1, keepdims=True)
    return xc * lax.rsqrt(var + LN_EPS) * g + b


def _silu(x):
    h = 0.5 * x
    return h + h * jnp.tanh(h)


def _expand(x, gs, mask):
    if gs == 1:
        return x
    return jnp.where(mask, jnp.concatenate([x] * gs, axis=1), jnp.zeros((), x.dtype))


def _col_bcast(rows, gs):
    outs = []
    for g in range(gs):
        outs.append(jnp.broadcast_to(rows[g:g + 1, :], (DK, DK)).T)
    return outs[0] if gs == 1 else jnp.concatenate(outs, axis=0)


def _boundary_rows(b, m):
    C, W = b.shape
    blk = 2 * m
    if blk >= 8:
        b3 = b.reshape(C // blk, blk, W)
        return jnp.broadcast_to(b3[:, m - 1:m, :], b3.shape).reshape(C, W)
    b3 = b.reshape(C // 8, 8, W)
    sub = lax.broadcasted_iota(jnp.int32, (1, 8, 1), 1)
    r = jnp.broadcast_to(b3[:, m - 1:m, :], b3.shape)
    for k in range(1, 8 // blk):
        r = jnp.where(sub >= k * blk, jnp.broadcast_to(b3[:, k * blk + m - 1:k * blk + m, :], b3.shape), r)
    return r.reshape(C, W)


def _mix_kernel(cfg, g_total, x_ref, eg_ref, eb_ref, win_ref, wo_ref, lb_ref, hgn_ref, rtg_ref, rtb_ref,
                l1g_ref, l1b_ref, cos_ref, sin_ref, a_ref, lv_ref, dec_ref, qdec_ref, kdec_ref,
                hg0_ref, rt0_ref, *rest):
    x1_ref, hg_ref, rt_ref = rest[-3:]
    C = cfg.C
    t = pl.program_id(1)

    @pl.when(t == 0)
    def _():
        hg_ref[...] = hg0_ref[...]
        rt_ref[...] = rt0_ref[...]

    x = x_ref[...]
    if cfg.embed:
        x = _ln(x, eg_ref[...], eb_ref[...])
    xb = x.astype(BF16)
    parts = [jnp.dot(xb, win_ref[:, k * HW:(k + 1) * HW], preferred_element_type=F32) for k in range(N_PARTS)]
    outs = []
    for s in range(cfg.sub):
        rows = slice(s * C, (s + 1) * C)
        outs.append(_mix_tile(cfg, g_total, [p[rows, :] for p in parts], cos_ref[rows, :], sin_ref[rows, :],
                              lb_ref, hgn_ref, rtg_ref, rtb_ref, a_ref, lv_ref, dec_ref, qdec_ref, kdec_ref,
                              hg_ref, rt_ref))
    o_all = outs[0] if cfg.sub == 1 else jnp.concatenate(outs, axis=0)
    mix = jnp.dot(o_all, wo_ref[...], preferred_element_type=F32)
    x1_ref[...] = _ln(ALPHA * x + mix, l1g_ref[...], l1b_ref[...])


def _mix_tile(cfg, g_total, parts, cos, sin, lb_ref, hgn_ref, rtg_ref, rtb_ref, a_ref, lv_ref, dec_ref,
              qdec_ref, kdec_ref, hg_ref, rt_ref):
    C, L, Lv = cfg.C, cfg.L, cfg.Lv
    gs = C // L
    levels = _levels(Lv)
    hq, hf, hv, hgate, rq, rk, rv, rgate = parts
    heads = [slice(h * DK, (h + 1) * DK) for h in range(HEADS)]

    if gs > 1:
        row_seq = lax.broadcasted_iota(jnp.int32, (C, gs * DK), 0) // L
        lane_blk = lax.broadcasted_iota(jnp.int32, (C, gs * DK), 1) // DK
        seq_mask = row_seq == lane_blk
    else:
        seq_mask = None
    lv = lv_ref[...]


    qdec, kdec = qdec_ref[...], kdec_ref[...]
    rt_q, rt_k, rt_sc = [], [], []
    for h, sl in enumerate(heads):
        qh = rq[:, sl] * cos + pltpu.roll(rq[:, sl], DK // 2, 1) * sin
        kh = (rk[:, sl] * cos + pltpu.roll(rk[:, sl], DK // 2, 1) * sin) * (DK ** -0.5)
        sc = lax.dot_general(qh.astype(BF16), kh.astype(BF16), NT_DIMS, preferred_element_type=F32)
        rt_q.append(qh)
        rt_k.append(kh)
        rt_sc.append((sc * dec_ref[h]).astype(BF16))

    lb = lb_ref[...]
    th = jnp.tanh(0.5 * hf)
    f_gate = lb + (1.0 - lb) * (0.5 + 0.5 * th)
    lf = jnp.log(jnp.maximum(f_gate, F_EPS))
    kx = (1.0 - lb) * (0.5 - 0.5 * th)
    q = _silu(hq)
    if Lv < L:
        real = (lax.broadcasted_iota(jnp.int32, (C, 1), 0) % L) < Lv
        lf = jnp.where(real, lf, 0.0)
        kx = jnp.where(real, kx, 0.0)
    lf_hi = lf.astype(BF16)
    rem = lf - lf_hi.astype(F32)
    lf_mid = rem.astype(BF16)
    lf_lo = (rem - lf_mid.astype(F32)).astype(BF16)
    sums = (jnp.dot(a_ref[...], jnp.concatenate([lf_hi, lf_mid], axis=0), preferred_element_type=F32)
            + jnp.dot(a_ref[:, 0:C], lf_lo, preferred_element_type=F32))
    b_incl = sums[0:C]
    q_in = q * jnp.exp(b_incl)
    k_out = kx * jnp.exp(sums[C:2 * C])
    if gs == 1:
        b_last = b_incl[Lv - 1:Lv, :]
    else:
        b_last = b_incl.reshape(gs, L, HW)[:, Lv - 1, :]
    d_rows = jnp.exp(b_last)

    qb, kb = q.astype(BF16), kx.astype(BF16)
    level_sc = [[lax.dot_general(qb[:, sl], kb[:, sl], NT_DIMS, preferred_element_type=F32)] for sl in heads]
    for m in levels:
        for h, sl in enumerate(heads):
            b_h = b_incl[:, sl]
            e = jnp.exp(-jnp.abs(b_h - _boundary_rows(b_h, m))).astype(BF16)
            level_sc[h].append(lax.dot_general(qb[:, sl] * e, kb[:, sl] * e, NT_DIMS,
                                               preferred_element_type=F32))

    outs_rt = []
    for h, sl in enumerate(heads):
        vb = rv[:, sl].astype(BF16)
        s_old = rt_ref[:, h].reshape(gs * DK, DK)
        q_st = _expand((rt_q[h] * qdec[:, sl]).astype(BF16), gs, seq_mask)
        o = jnp.dot(jnp.concatenate([rt_sc[h], q_st], axis=1),
                    jnp.concatenate([vb, s_old.astype(BF16)], axis=0), preferred_element_type=F32)
        upd = lax.dot_general(_expand((rt_k[h] * kdec[:, sl]).astype(BF16), gs, seq_mask), vb, TN_DIMS,
                              preferred_element_type=F32)
        rt_ref[:, h] = (g_total[h] * s_old + upd).reshape(gs, DK, DK)
        mu = jnp.mean(o, -1, keepdims=True)
        oc = o - mu
        o = oc * lax.rsqrt(jnp.mean(oc * oc, -1, keepdims=True) + LN_EPS)
        outs_rt.append((o * rtg_ref[:, sl] + rtb_ref[:, sl]) * _silu(rgate[:, sl]))

    outs_hg = []
    for h, sl in enumerate(heads):
        vb = hv[:, sl].astype(BF16)
        s_old = hg_ref[:, h].reshape(gs * DK, DK)
        q_st = _expand(q_in[:, sl].astype(BF16), gs, seq_mask)
        scores = jnp.where(lv == 0, level_sc[h][0], 0.0)
        for j in range(len(levels)):
            scores = jnp.where(lv == j + 1, level_sc[h][j + 1], scores)
        o = jnp.dot(jnp.concatenate([scores.astype(BF16), q_st], axis=1),
                    jnp.concatenate([vb, s_old.astype(BF16)], axis=0), preferred_element_type=F32)
        upd = lax.dot_general(_expand(k_out[:, sl].astype(BF16), gs, seq_mask), vb, TN_DIMS,
                              preferred_element_type=F32)
        hg_ref[:, h] = (_col_bcast(d_rows[:, sl], gs) * s_old + upd).reshape(gs, DK, DK)
        o = o * lax.rsqrt(jnp.mean(o * o, -1, keepdims=True) + LN_EPS)
        outs_hg.append(o * hgn_ref[:, sl] * _silu(hgate[:, sl]))

    return jnp.concatenate(outs_hg + outs_rt, axis=1).astype(BF16)


def _ffn_kernel(cfg, x_ref, wa_ref, wb_ref, cw_ref, cb_ref, wd_ref, l2g_ref, l2b_ref, *rest):
    C, L, Lv = cfg.C, cfg.L, cfg.Lv
    gs = C // L
    if cfg.per_seq:
        buf_ref, y_ref, cs_ref, a_scr = rest
    else:
        init_ref, y_ref, cs_ref, a_scr = rest
    t = pl.program_id(1)

    x = x_ref[...]
    xb = x.astype(BF16)
    if cfg.per_seq:
        pos = lax.broadcasted_iota(jnp.int32, (C, 1), 0) % L
    else:
        @pl.when(t == 0)
        def _():
            a_scr[...] = jnp.zeros((CARRY, D_FF), F32)
            a_scr[CARRY - 2:CARRY, :] = init_ref[0]
        row8 = lax.broadcasted_iota(jnp.int32, (CARRY, 1), 0)

    def up_proj(j):
        cs = slice(j * FF_CHUNK, (j + 1) * FF_CHUNK)
        return (jnp.dot(xb, wa_ref[:, cs], preferred_element_type=F32),
                jnp.dot(xb, wb_ref[:, cs], preferred_element_type=F32))

    n_chunks = D_FF // FF_CHUNK
    hidden = []
    ahead = up_proj(0)
    for j in range(n_chunks):
        cs = slice(j * FF_CHUNK, (j + 1) * FF_CHUNK)
        a, gate = ahead
        if j + 1 < n_chunks:
            ahead = up_proj(j + 1)
        a_m1 = pltpu.roll(a, 1, 0)
        a_m2 = pltpu.roll(a, 2, 0)
        if cfg.per_seq:
            older = jnp.broadcast_to(buf_ref[:, 0:1, cs], (gs, L, FF_CHUNK)).reshape(C, FF_CHUNK)
            newer = jnp.broadcast_to(buf_ref[:, 1:2, cs], (gs, L, FF_CHUNK)).reshape(C, FF_CHUNK)
            a_m1 = jnp.where(pos >= 1, a_m1, newer)
            a_m2 = jnp.where(pos >= 2, a_m2, jnp.where(pos == 1, newer, older))
        else:
            prev = a_scr[:, cs]
            head1 = jnp.where(row8 == 0, prev[CARRY - 1:CARRY, :], a_m1[0:CARRY, :])
            head2 = jnp.where(row8 == 0, prev[CARRY - 2:CARRY - 1, :],
                              jnp.where(row8 == 1, prev[CARRY - 1:CARRY, :], a_m2[0:CARRY, :]))
            a_m1 = jnp.concatenate([head1, a_m1[CARRY:, :]], axis=0)
            a_m2 = jnp.concatenate([head2, a_m2[CARRY:, :]], axis=0)
            a_scr[:, cs] = a[C - CARRY:, :]
        conv = cb_ref[:, cs] + cw_ref[0:1, cs] * a_m2 + cw_ref[1:2, cs] * a_m1 + cw_ref[2:3, cs] * a
        th = jnp.tanh(conv * (GELU_K0 + GELU_K1 * (conv * conv)))
        hidden.append(((conv + conv * th) * gate).astype(BF16))
        if cfg.per_seq:
            a_seq = a.reshape(gs, L, FF_CHUNK)
            cs_ref[:, :, cs] = a_seq[:, Lv - 2:Lv, :]
        else:
            cs_ref[0, :, cs] = a[Lv - 2:Lv, :]
    ffn = jnp.dot(jnp.concatenate(hidden, axis=1), wd_ref[...], preferred_element_type=F32)
    y_ref[...] = _ln(ALPHA * x + ffn, l2g_ref[...], l2b_ref[...])


def _const_spec(shape):
    nd = len(shape)
    return pl.BlockSpec(shape, lambda b, t: (0,) * nd, pipeline_mode=pl.Buffered(1))


def _layer_spec(shape, layer):
    nd = len(shape)
    return pl.BlockSpec((None,) + tuple(shape), lambda b, t: (layer,) + (0,) * nd,
                        pipeline_mode=pl.Buffered(1))


def _params():
    return pltpu.CompilerParams(dimension_semantics=("arbitrary", "arbitrary"),
                                vmem_limit_bytes=VMEM_LIMIT_V7X)


def _mix_call(cfg, x, eg, eb, win, wo, lb, hgn, rtg, rtb, l1g, l1b, cos, sin, hg0, rt0, stacked=None):
    C, L, Lv, nb, nt = cfg.C, cfg.L, cfg.Lv, cfg.nb, cfg.nt
    gs = C // L
    a_stack, lv, dec, qdec, kdec, g_total = _tables(C, L, Lv)
    a_stack = jnp.asarray(a_stack, BF16)
    l = cfg.layer
    step_rows = cfg.sub * C
    rows = nb * nt * step_rows
    assert cfg.sub == 1 or gs == 1
    row_spec = pl.BlockSpec((step_rows, D_MODEL), lambda b, t: (b * nt + t, 0))
    if cfg.per_seq:
        st_spec = pl.BlockSpec((None, gs, HEADS, DK, DK), lambda b, t: (l, b, 0, 0, 0))
    else:
        st_spec = pl.BlockSpec((gs, HEADS, DK, DK), lambda b, t: (0, 0, 0, 0))
    in_specs = [
        row_spec,
        _const_spec((1, D_MODEL)), _const_spec((1, D_MODEL)),
        _layer_spec((D_MODEL, PROJ_COLS), l), _layer_spec((D_MODEL, D_MODEL), l),
        _layer_spec((1, HW), l), _layer_spec((1, HW), l), _layer_spec((1, HW), l), _layer_spec((1, HW), l),
        _layer_spec((1, D_MODEL), l), _layer_spec((1, D_MODEL), l),
        pl.BlockSpec((step_rows, DK), lambda b, t: (t, 0)), pl.BlockSpec((step_rows, DK), lambda b, t: (t, 0)),
        _const_spec(a_stack.shape), _const_spec(lv.shape), _const_spec(dec.shape),
        _const_spec(qdec.shape), _const_spec(kdec.shape),
        st_spec, st_spec,
    ]
    operands = [x, eg, eb, win, wo, lb, hgn, rtg, rtb, l1g, l1b, cos, sin,
                a_stack, jnp.asarray(lv), jnp.asarray(dec), jnp.asarray(qdec), jnp.asarray(kdec), hg0, rt0]
    aliases = {}
    if cfg.per_seq:
        out_st = pl.BlockSpec((None, gs, HEADS, DK, DK), lambda b, t: (l, b, 0, 0, 0))
        st_shape = jax.ShapeDtypeStruct((DEPTH, nb * gs, HEADS, DK, DK), F32)
        if stacked is not None:
            aliases = {len(operands): 1, len(operands) + 1: 2}
            in_specs += [pl.BlockSpec(memory_space=pl.ANY), pl.BlockSpec(memory_space=pl.ANY)]
            operands += list(stacked)
    else:
        out_st = pl.BlockSpec((gs, HEADS, DK, DK), lambda b, t: (b, 0, 0, 0))
        st_shape = jax.ShapeDtypeStruct((nb * gs, HEADS, DK, DK), F32)
    return pl.pallas_call(
        functools.partial(_mix_kernel, cfg, g_total),
        out_shape=(jax.ShapeDtypeStruct((rows, D_MODEL), F32), st_shape, st_shape),
        grid=(nb, nt),
        in_specs=in_specs,
        out_specs=(row_spec, out_st, out_st),
        input_output_aliases=aliases,
        compiler_params=_params(),
        name=f"mixer_l{l}_L{L}_nt{nt}",
    )(*operands)


def _ffn_call(cfg, x, wa, wb, cw, cb, wd, l2g, l2b, *conv_in):
    C, L, nb, nt = cfg.C, cfg.L, cfg.nb, cfg.nt
    gs = C // L
    l = cfg.layer
    rows = nb * nt * C
    row_spec = pl.BlockSpec((C, D_MODEL), lambda b, t: (b * nt + t, 0))
    in_specs = [
        row_spec,
        _layer_spec((D_MODEL, D_FF), l), _layer_spec((D_MODEL, D_FF), l),
        _layer_spec((CONV_W, D_FF), l), _layer_spec((1, D_FF), l),
        _layer_spec((D_FF, D_MODEL), l),
        _layer_spec((1, D_MODEL), l), _layer_spec((1, D_MODEL), l),
    ]
    if cfg.per_seq:
        in_specs += [pl.BlockSpec((None, gs, CONV_W - 1, D_FF), lambda b, t: (l, b, 0, 0))]
        cs_shape = jax.ShapeDtypeStruct((nb * gs, CONV_W - 1, D_FF), F32)
        cs_spec = pl.BlockSpec((gs, CONV_W - 1, D_FF), lambda b, t: (b, 0, 0))
    else:
        in_specs += [pl.BlockSpec((1, CONV_W - 1, D_FF), lambda b, t: (0, 0, 0))]
        cs_shape = jax.ShapeDtypeStruct((nb, CONV_W - 1, D_FF), F32)
        cs_spec = pl.BlockSpec((1, CONV_W - 1, D_FF), lambda b, t: (b, 0, 0))
    return pl.pallas_call(
        functools.partial(_ffn_kernel, cfg),
        out_shape=(jax.ShapeDtypeStruct((rows, D_MODEL), F32), cs_shape),
        grid=(nb, nt),
        in_specs=in_specs,
        out_specs=(row_spec, cs_spec),
        scratch_shapes=[pltpu.VMEM((CARRY, D_FF), F32)],
        compiler_params=_params(),
        name=f"convffn_l{l}_L{L}_nt{nt}",
    )(x, wa, wb, cw, cb, wd, l2g, l2b, *conv_in)


def _rope_tables(pos):
    half = DK // 2
    inv = ROPE_BASE ** (-jnp.arange(half, dtype=F32) / half)
    ang = pos.astype(F32)[:, None] * inv[None, :]
    cos, sin = jnp.cos(ang), jnp.sin(ang)
    cos2 = jnp.concatenate([cos, cos], axis=1)
    sin2 = jnp.concatenate([-sin, sin], axis=1)
    return cos2, sin2


def kernel(x_prompt, x_sample, state_hgrn, state_ret, state_conv, meta_tokens, emb_ln_g, emb_ln_b, lb_logits, w_in, hg_norm_g, rt_gn_g, rt_gn_b, w_o, ln1_g, ln1_b, w_a, w_b, conv_w, conv_b, w_down, ln2_g, ln2_b):
    bsz, seq, _ = x_prompt.shape
    dec_b, dec_t, _ = x_sample.shape
    C = TILE
    assert seq % (C * MIX_SUB) == 0 and seq % FFN_TILE == 0 and N_META <= C and dec_t >= CONV_W - 1
    L_s = 8
    assert dec_t <= L_s and (dec_b * L_s) % C == 0

    sm = jax.nn.softmax(lb_logits.astype(F32), axis=0)
    lb_all = (jnp.cumsum(sm, axis=0) - sm[0:1])[:, None, :]
    win, wo = w_in.astype(BF16), w_o.astype(BF16)
    wa, wb, wd = w_a.astype(BF16), (0.5 * w_b).astype(BF16), w_down.astype(BF16)
    vec = lambda v: v[:, None, :]
    hgn, rtg, rtb = vec(hg_norm_g), vec(rt_gn_g), vec(rt_gn_b)
    l1g, l1b, l2g, l2b, cb = vec(ln1_g), vec(ln1_b), vec(ln2_g), vec(ln2_b), vec(conv_b)
    eg, eb = emb_ln_g[None, :], emb_ln_b[None, :]

    cos_m, sin_m = _rope_tables(jnp.arange(C, dtype=jnp.int32))
    cos_p, sin_p = _rope_tables(N_META + jnp.arange(seq, dtype=jnp.int32))
    cos_s, sin_s = _rope_tables(PAST_LEN + jnp.arange(C, dtype=jnp.int32) % L_s)

    xm = jnp.pad(meta_tokens.astype(F32), ((0, C - N_META), (0, 0)))
    xp = x_prompt.reshape(bsz * seq, D_MODEL)
    xs = jnp.pad(x_sample, ((0, 0), (0, L_s - dec_t), (0, 0))).reshape(dec_b * L_s, D_MODEL)
    zero_state = jnp.zeros((1, HEADS, DK, DK), F32)
    zero_conv = jnp.zeros((1, CONV_W - 1, D_FF), F32)

    hg_p, rt_p, cv_p, cv_s = [], [], [], []
    st_s = (jnp.zeros_like(state_hgrn, dtype=F32), jnp.zeros_like(state_ret, dtype=F32))
    for l in range(DEPTH):
        emb = l == 0
        cfg_m = Cfg(1, 1, C, C, N_META, emb, False, l)
        cfg_p = Cfg(bsz, seq // (C * MIX_SUB), C, C, C, emb, False, l, MIX_SUB)
        cfg_s = Cfg(dec_b * L_s // C, 1, C, L_s, dec_t, emb, True, l)
        mix_w = (eg, eb, win, wo, lb_all, hgn, rtg, rtb, l1g, l1b)
        ffn_w = (wa, wb, conv_w, cb, wd, l2g, l2b)

        xm1, hg_m, rt_m = _mix_call(cfg_m, xm, *mix_w, cos_m, sin_m, zero_state, zero_state)
        xm, cv_m = _ffn_call(cfg_m, xm1, *ffn_w, zero_conv)

        xp1, hg, rt = _mix_call(cfg_p, xp, *mix_w, cos_p, sin_p, hg_m, rt_m)
        cfg_pf = Cfg(bsz, seq // FFN_TILE, FFN_TILE, FFN_TILE, FFN_TILE, False, False, l)
        xp, cv = _ffn_call(cfg_pf, xp1, *ffn_w, cv_m)
        hg_p.append(hg)
        rt_p.append(rt)
        cv_p.append(cv)

        xs1, *st_s = _mix_call(cfg_s, xs, *mix_w, cos_s, sin_s, state_hgrn, state_ret, stacked=st_s)
        cfg_sf = Cfg(dec_b * L_s // FFN_TILE_SAMPLE, 1, FFN_TILE_SAMPLE, L_s, dec_t, False, True, l)
        xs, cv = _ffn_call(cfg_sf, xs1, *ffn_w, state_conv)
        cv_s.append(cv)

    y_prompt = xp.reshape(bsz, seq, D_MODEL)
    y_sample = xs.reshape(dec_b, L_s, D_MODEL)[:, :dec_t]
    hg_s, rt_s = st_s
    return (y_prompt, y_sample, jnp.stack(hg_p), hg_s, jnp.stack(rt_p), rt_s, jnp.stack(cv_p), jnp.stack(cv_s))
```
